```python
import jax, jax.numpy as jnp
from jax import lax
import numpy as np

D_MODEL = 1024
BATCH = 8
SEQ = 4096
DEPTH = 2
DEC_BATCH = 32
DEC_SEQ = 8
PAST_LEN = 16384
PAGE_SIZE = 128

FOX_HEADS = 8
FOX_HEAD_DIM = 64
FOX_WIDTH = FOX_HEADS * FOX_HEAD_DIM
HG_HEADS = 4
HG_KEY_DIM = 128
HG_VAL_DIM = 128
HG_WIDTH = HG_HEADS * HG_KEY_DIM
HG_VWIDTH = HG_HEADS * HG_VAL_DIM
D_FF = 2816
Q_BLOCK = 128
GLA_CHUNK = 64
EPS = 1e-6
FOX_F_BIAS_INIT = 8.0
SPLIT_SIZES = (FOX_WIDTH, FOX_WIDTH, FOX_WIDTH, FOX_HEADS, HG_WIDTH, HG_WIDTH, HG_VWIDTH, HG_VWIDTH, D_MODEL, D_MODEL)
IN_COLS = 3 * FOX_WIDTH + FOX_HEADS + 2 * HG_WIDTH + 2 * HG_VWIDTH + 2 * D_MODEL

kernel_name = 'fox_hgrn2_macaron_decode_step'


def rmsnorm(x, gain):
    xf = x.astype(jnp.float32)
    y = xf * lax.rsqrt(jnp.mean(xf * xf, axis=-1, keepdims=True) + EPS)
    return (y * gain.astype(jnp.float32)).astype(x.dtype)


def swiglu(x, w_in, w_out):
    a, b = jnp.split(x @ w_in, 2, axis=-1)
    return (jax.nn.silu(a) * b) @ w_out


def split_cols(p):
    cuts = np.cumsum(np.array(SPLIT_SIZES))[:-1].tolist()
    return jnp.split(p, cuts, axis=-1)


def mixer_inputs(h, w_in, fox_f_bias, q_gain, k_gain, lb):
    b, l, _ = h.shape
    fq, fk, fv, ff, hq, hf, hi, hgt, ga, gb = split_cols(h @ w_in)
    fox_q = rmsnorm(fq.reshape(b, l, FOX_HEADS, FOX_HEAD_DIM), q_gain)
    fox_k = rmsnorm(fk.reshape(b, l, FOX_HEADS, FOX_HEAD_DIM), k_gain)
    fox_v = fv.reshape(b, l, FOX_HEADS, FOX_HEAD_DIM)
    fox_logf = jax.nn.log_sigmoid((ff + fox_f_bias).astype(jnp.float32))
    hg_q = jax.nn.silu(hq).reshape(b, l, HG_HEADS, HG_KEY_DIM)
    lbh = lb.reshape(HG_HEADS, HG_KEY_DIM)
    hg_logf = jnp.logaddexp(jnp.log(lbh), jnp.log1p(-lbh) + jax.nn.log_sigmoid(hf.astype(jnp.float32).reshape(b, l, HG_HEADS, HG_KEY_DIM)))
    hg_k = -jnp.expm1(hg_logf)
    hg_v = hi.reshape(b, l, HG_HEADS, HG_VAL_DIM)
    hg_gate = hgt.reshape(b, l, HG_HEADS, HG_VAL_DIM)
    return (fox_q, fox_k, fox_v, fox_logf, hg_q, hg_k, hg_v, hg_logf, hg_gate, jax.nn.sigmoid(ga), jax.nn.sigmoid(gb))


def fox_prompt(q, k, v, logf):
    b, l, h, dh = q.shape
    qb = min(Q_BLOCK, l)
    nblk = l // qb
    scale = dh ** -0.5
    c_keys = jnp.transpose(jnp.cumsum(logf.astype(jnp.float32), axis=1), (0, 2, 1))
    q_blocks = jnp.moveaxis(q.reshape(b, nblk, qb, h, dh), 1, 0)
    c_blocks = jnp.moveaxis(c_keys.reshape(b, h, nblk, qb), 2, 0)
    kpos = jnp.arange(l)

    def one_block(args):
        qi, ci, i = args
        s = jnp.einsum('bthd,bshd->bhts', qi, k).astype(jnp.float32) * scale
        s = s + ci[..., None] - c_keys[:, :, None, :]
        qpos = i * qb + jnp.arange(qb)
        s = jnp.where(kpos[None, :] <= qpos[:, None], s, -jnp.inf)
        p = jax.nn.softmax(s, axis=-1)
        return jnp.einsum('bhts,bshd->bthd', p.astype(v.dtype), v)

    o = lax.map(one_block, (q_blocks, c_blocks, jnp.arange(nblk)))
    return jnp.moveaxis(o, 0, 1).reshape(b, l, h, dh)


def fox_sample(q, k, v, logf, k_past, v_past, logf_past):
    dh = q.shape[-1]
    t = q.shape[1]
    p_len = k_past.shape[1]
    scale = dh ** -0.5
    lp = logf_past.astype(jnp.float32)
    suffix = lax.cumsum(lp, axis=1, reverse=True) - lp
    ct = jnp.transpose(jnp.cumsum(logf.astype(jnp.float32), axis=1), (0, 2, 1))
    s_past = jnp.einsum('bthd,bshd->bhts', q, k_past).astype(jnp.float32) * scale
    s_past = s_past + ct[..., None] + jnp.transpose(suffix, (0, 2, 1))[:, :, None, :]
    s_self = jnp.einsum('bthd,bshd->bhts', q, k).astype(jnp.float32) * scale
    s_self = s_self + ct[..., None] - ct[:, :, None, :]
    causal = jnp.tril(jnp.ones((t, t), dtype=bool))
    s_self = jnp.where(causal, s_self, -jnp.inf)
    p = jax.nn.softmax(jnp.concatenate([s_past, s_self], axis=-1), axis=-1)
    return (jnp.einsum('bhts,bshd->bthd', p[..., :p_len].astype(v.dtype), v_past)
            + jnp.einsum('bhts,bshd->bthd', p[..., p_len:].astype(v.dtype), v))


def gla_chunked(q, k, v, log_f, s0):
    b, l, h, kd = q.shape
    vd = v.shape[-1]
    c = min(GLA_CHUNK, l)
    pad = (-l) % c
    f32 = jnp.float32
    q = q.astype(f32) * (kd ** -0.5)
    k = k.astype(f32)
    v = v.astype(f32)
    log_f = log_f.astype(f32)
    if pad:
        pw = ((0, 0), (0, pad), (0, 0), (0, 0))
        q, k, v, log_f = jnp.pad(q, pw), jnp.pad(k, pw), jnp.pad(v, pw), jnp.pad(log_f, pw)
    n = (l + pad) // c

    def to_chunks(a):
        return jnp.moveaxis(a.reshape(b, n, c, h, a.shape[-1]), 1, 0)

    causal = jnp.tril(jnp.ones((c, c), dtype=bool))

    def step(S, inp):
        qc, kc, vc, gc = inp
        bcum = jnp.cumsum(gc, axis=1)
        diff = bcum[:, :, None] - bcum[:, None, :]
        decay = jnp.exp(jnp.where(causal[None, :, :, None, None], diff, -jnp.inf))
        scores = jnp.einsum('bthk,btshk,bshk->bhts', qc, decay, kc)
        o = (jnp.einsum('bhts,bshv->bthv', scores, vc)
             + jnp.einsum('bthk,bhkv->bthv', qc * jnp.exp(bcum), S))
        b_last = bcum[:, -1]
        S = (jnp.exp(b_last)[..., None] * S
             + jnp.einsum('bshk,bshv->bhkv', kc * jnp.exp(b_last[:, None] - bcum), vc))
        return S, o

    S, o = lax.scan(step, s0.astype(f32), (to_chunks(q), to_chunks(k), to_chunks(v), to_chunks(log_f)))
    o = jnp.moveaxis(o, 0, 1).reshape(b, n * c, h, vd)[:, :l]
    return o, S.astype(s0.dtype)


def merge_branches(o_fox, o_hg, hg_gate, gate_fox, gate_hg, o_gain, w_proj_fox, w_proj_hg, w_out):
    b, l = o_fox.shape[:2]
    y_fox = o_fox.reshape(b, l, FOX_WIDTH) @ w_proj_fox
    ob = rmsnorm(o_hg, o_gain) * jax.nn.silu(hg_gate)
    y_hg = ob.reshape(b, l, HG_VWIDTH) @ w_proj_hg
    return (gate_fox * y_fox + gate_hg * y_hg) @ w_out


def setup_inputs(seed: int = 0) -> dict:
    key = jax.random.key(seed)
    ks = jax.random.split(key, 32)
    f32 = jnp.float32
    n_pages = PAST_LEN // PAGE_SIZE
    n_used = DEC_BATCH * n_pages
    n_phys = n_used + max(1, n_used // 4)

    def nrm(k, shape, scale):
        return jax.random.normal(k, shape, f32) * scale

    def gain(k, shape):
        return 1.0 + 0.02 * jax.random.normal(k, shape, f32)

    page_table = jax.random.permutation(ks[6], n_phys)[:n_used].reshape(DEC_BATCH, n_pages).astype(jnp.int32)
    return {
        'x_prompt': nrm(ks[0], (BATCH, SEQ, D_MODEL), 1.0),
        'x_sample': nrm(ks[1], (DEC_BATCH, DEC_SEQ, D_MODEL), 1.0),
        'cache_k': nrm(ks[2], (DEPTH, n_phys, PAGE_SIZE, FOX_HEADS, FOX_HEAD_DIM), 1.0),
        'cache_v': nrm(ks[3], (DEPTH, n_phys, PAGE_SIZE, FOX_HEADS, FOX_HEAD_DIM), 1.0),
        'cache_logf': jax.nn.log_sigmoid(FOX_F_BIAS_INIT + nrm(ks[4], (DEPTH, n_phys, PAGE_SIZE, FOX_HEADS), 1.0)),
        'state_hgrn': nrm(ks[5], (DEPTH, DEC_BATCH, HG_HEADS, HG_KEY_DIM, HG_VAL_DIM), 0.5),
        'page_table': page_table,
        'ffn1_norm': gain(ks[7], (DEPTH, D_MODEL)),
        'ffn1_w_in': nrm(ks[8], (DEPTH, D_MODEL, 2 * D_FF), D_MODEL ** -0.5),
        'ffn1_w_out': nrm(ks[9], (DEPTH, D_FF, D_MODEL), D_FF ** -0.5),
        'mix_norm': gain(ks[10], (DEPTH, D_MODEL)),
        'w_in': nrm(ks[11], (DEPTH, D_MODEL, IN_COLS), D_MODEL ** -0.5),
        'fox_f_bias': FOX_F_BIAS_INIT + nrm(ks[12], (DEPTH, FOX_HEADS), 0.1),
        'fox_q_gain': gain(ks[13], (DEPTH, FOX_HEAD_DIM)),
        'fox_k_gain': gain(ks[14], (DEPTH, FOX_HEAD_DIM)),
        'hg_lb_logits': nrm(ks[15], (DEPTH, HG_WIDTH), 0.5),
        'hg_o_gain': gain(ks[16], (DEPTH, HG_VAL_DIM)),
        'w_proj_fox': nrm(ks[17], (DEPTH, FOX_WIDTH, D_MODEL), FOX_WIDTH ** -0.5),
        'w_proj_hg': nrm(ks[18], (DEPTH, HG_VWIDTH, D_MODEL), HG_VWIDTH ** -0.5),
        'w_out': nrm(ks[19], (DEPTH, D_MODEL, D_MODEL), D_MODEL ** -0.5),
        'ffn2_norm': gain(ks[20], (DEPTH, D_MODEL)),
        'ffn2_w_in': nrm(ks[21], (DEPTH, D_MODEL, 2 * D_FF), D_MODEL ** -0.5),
        'ffn2_w_out': nrm(ks[22], (DEPTH, D_FF, D_MODEL), D_FF ** -0.5),
    }


def reference(x_prompt, x_sample, cache_k, cache_v, cache_logf, state_hgrn, page_table,
              ffn1_norm, ffn1_w_in, ffn1_w_out, mix_norm, w_in, fox_f_bias, fox_q_gain, fox_k_gain,
              hg_lb_logits, hg_o_gain, w_proj_fox, w_proj_hg, w_out, ffn2_norm, ffn2_w_in, ffn2_w_out):
    dec_b, n_pages = page_table.shape
    lb_all = jnp.cumsum(jax.nn.softmax(hg_lb_logits.astype(jnp.float32), axis=0), axis=0)
    lb_all = lb_all - lb_all[0:1]
    yp, ys = x_prompt, x_sample
    kp_l, vp_l, fp_l, sp_l, ks_l, vs_l, fs_l, ss_l = [], [], [], [], [], [], [], []
    for l in range(DEPTH):
        yp = yp + 0.5 * swiglu(rmsnorm(yp, ffn1_norm[l]), ffn1_w_in[l], ffn1_w_out[l])
        ys = ys + 0.5 * swiglu(rmsnorm(ys, ffn1_norm[l]), ffn1_w_in[l], ffn1_w_out[l])

        fq, fk, fv, flf, hq, hk, hv, hlf, hgt, ga, gb = mixer_inputs(
            rmsnorm(yp, mix_norm[l]), w_in[l], fox_f_bias[l], fox_q_gain[l], fox_k_gain[l], lb_all[l])
        o_fox = fox_prompt(fq, fk, fv, flf)
        s0 = jnp.zeros((yp.shape[0], HG_HEADS, HG_KEY_DIM, HG_VAL_DIM), yp.dtype)
        o_hg, s_new = gla_chunked(hq, hk, hv, hlf, s0)
        yp = yp + merge_branches(o_fox, o_hg.astype(yp.dtype), hgt, ga, gb, hg_o_gain[l], w_proj_fox[l], w_proj_hg[l], w_out[l])
        kp_l.append(fk)
        vp_l.append(fv)
        fp_l.append(flf.astype(cache_logf.dtype))
        sp_l.append(s_new)

        fq, fk, fv, flf, hq, hk, hv, hlf, hgt, ga, gb = mixer_inputs(
            rmsnorm(ys, mix_norm[l]), w_in[l], fox_f_bias[l], fox_q_gain[l], fox_k_gain[l], lb_all[l])
        k_past = cache_k[l][page_table].reshape(dec_b, n_pages * PAGE_SIZE, FOX_HEADS, FOX_HEAD_DIM)
        v_past = cache_v[l][page_table].reshape(dec_b, n_pages * PAGE_SIZE, FOX_HEADS, FOX_HEAD_DIM)
        lf_past = cache_logf[l][page_table].reshape(dec_b, n_pages * PAGE_SIZE, FOX_HEADS)
        o_fox = fox_sample(fq, fk, fv, flf, k_past, v_past, lf_past)
        o_hg, s_new = gla_chunked(hq, hk, hv, hlf, state_hgrn[l])
        ys = ys + merge_branches(o_fox, o_hg.astype(ys.dtype), hgt, ga, gb, hg_o_gain[l], w_proj_fox[l], w_proj_hg[l], w_out[l])
        ks_l.append(fk)
        vs_l.append(fv)
        fs_l.append(flf.astype(cache_logf.dtype))
        ss_l.append(s_new)

        yp = yp + 0.5 * swiglu(rmsnorm(yp, ffn2_norm[l]), ffn2_w_in[l], ffn2_w_out[l])
        ys = ys + 0.5 * swiglu(rmsnorm(ys, ffn2_norm[l]), ffn2_w_in[l], ffn2_w_out[l])

    return (yp, ys, jnp.stack(kp_l), jnp.stack(vp_l), jnp.stack(fp_l), jnp.stack(sp_l),
            jnp.stack(ks_l), jnp.stack(vs_l), jnp.stack(fs_l), jnp.stack(ss_l))
```

```python
import functools

import jax
import jax.numpy as jnp
import numpy as np
from jax import lax
from jax.experimental import pallas as pl
from jax.experimental.pallas import tpu as pltpu

F32 = jnp.float32
BF16 = jnp.bfloat16
EPS = 1e-6
NEG_INF = float("-inf")

V7X_LANES = 128
V7X_SUBLANES = 8
V7X_VMEM_BYTES = 64 * 1024 * 1024
VMEM_LIMIT = (V7X_VMEM_BYTES * 3) // 4

ROW_TILE = 512
ATTN_TILE = 512
GLA_CHUNK = 64
GLA_LEAF = V7X_SUBLANES
GLA_ROWS = 512
PAGES_PER_STEP = 8
SUFFIX_PAGES_PER_STEP = 16


def _params(*sem):
    return pltpu.CompilerParams(dimension_semantics=sem, vmem_limit_bytes=VMEM_LIMIT)


def _resident(shape):
    nd = len(shape)
    return pl.BlockSpec(shape, lambda *_: (0,) * nd, pipeline_mode=pl.Buffered(1))


def _dot(a, b):
    return jnp.dot(a, b, preferred_element_type=F32)


def _dot_nt(a, b):
    return lax.dot_general(a, b, (((1,), (1,)), ((), ())), preferred_element_type=F32)


def _dot_tn(a, b):
    return lax.dot_general(a, b, (((0,), (0,)), ((), ())), preferred_element_type=F32)


def _split3(x):
    hi = x.astype(BF16)
    r1 = x - hi.astype(F32)
    mid = r1.astype(BF16)
    lo = (r1 - mid.astype(F32)).astype(BF16)
    return hi, mid, lo


def _dot01(m01, x):
    hi, mid, lo = _split3(x)
    return _dot(m01, hi) + _dot(m01, mid) + _dot(m01, lo)


def _dot01_nt(m01, x):
    hi, mid, lo = _split3(x)
    return _dot_nt(m01, hi) + _dot_nt(m01, mid) + _dot_nt(m01, lo)


def _dot01_tn(x, m01):
    hi, mid, lo = _split3(x)
    return _dot_tn(hi, m01) + _dot_tn(mid, m01) + _dot_tn(lo, m01)


def _rms_rows(x, gain):
    return x * lax.rsqrt(jnp.mean(x * x, axis=-1, keepdims=True) + EPS) * gain


def _sigmoid(x):
    return 1.0 / (1.0 + jnp.exp(-x))


def _ffn_body(x_ref, g_ref, wi_ref, wo_ref, o_ref, acc_ref, *, d_ff, tf):
    x = x_ref[...]
    h = _rms_rows(x, g_ref[...]).astype(BF16)
    for j in range(d_ff // tf):
        a = _dot(h, wi_ref[:, j * tf:(j + 1) * tf])
        b = _dot(h, wi_ref[:, d_ff + j * tf:d_ff + (j + 1) * tf])
        g = (a * _sigmoid(a) * b).astype(BF16)
        y = _dot(g, wo_ref[j * tf:(j + 1) * tf, :])
        if j == 0:
            acc_ref[...] = y
        else:
            acc_ref[...] += y
    o_ref[...] = x + 0.5 * acc_ref[...]


def _ffn(x, gain, w_in, w_out):
    n, d = x.shape
    d_ff = w_out.shape[0]
    tm = min(ROW_TILE, n)
    tf = 2 * V7X_LANES if d_ff % (2 * V7X_LANES) == 0 else V7X_LANES
    assert n % tm == 0 and d_ff % tf == 0
    return pl.pallas_call(
        functools.partial(_ffn_body, d_ff=d_ff, tf=tf),
        grid=(n // tm,),
        in_specs=[pl.BlockSpec((tm, d), lambda i: (i, 0)),
                  _resident((1, d)),
                  _resident((d, 2 * d_ff)),
                  _resident((d_ff, d))],
        out_specs=pl.BlockSpec((tm, d), lambda i: (i, 0)),
        out_shape=jax.ShapeDtypeStruct((n, d), F32),
        scratch_shapes=[pltpu.VMEM((tm, d), F32)],
        compiler_params=_params("parallel"),
        name="ffn",
    )(x, gain, w_in, w_out)


def _inproj_body(x_ref, g_ref, w_ref, gsum_ref, qg_ref, kg_ref, fb_ref, lbl_ref,
                 q_ref, kf_ref, vf_ref, kb_ref, vb_ref, lf_ref,
                 hq_ref, hk_ref, hlf_ref, hv_ref, hg_ref, *, fw, hw, hvw, dh, layer):
    h = _rms_rows(x_ref[...], g_ref[...]).astype(BF16)

    def proj(c0, n):
        return _dot(h, w_ref[:, c0:c0 + n])

    gsum = gsum_ref[...]

    def head_norm(z, gain):
        ms = _dot((z * z).astype(BF16), gsum) * (1.0 / dh)
        return z * lax.rsqrt(ms + EPS) * gain

    q = head_norm(proj(0, fw), qg_ref[...])
    q_ref[...] = (q * (dh ** -0.5)).astype(BF16)
    k = head_norm(proj(fw, fw), kg_ref[...])
    kf_ref[...] = k
    kb_ref[...] = k.astype(BF16)
    v = proj(2 * fw, fw)
    vf_ref[...] = v
    vb_ref[...] = v.astype(BF16)

    c0 = 3 * fw
    hq = proj(c0, hw)
    hq_ref[...] = hq * _sigmoid(hq)

    lg = lbl_ref[...]
    e = jnp.exp(lg - jnp.max(lg, axis=0, keepdims=True))
    sm = e / jnp.sum(e, axis=0, keepdims=True)
    lb = jnp.zeros_like(sm[0:1])
    for i in range(1, layer + 1):
        lb = lb + sm[i:i + 1]
    log_lb = jnp.log(lb)
    log_1m_lb = jnp.log1p(-lb)

    z = proj(c0 + hw, hw)
    ez = jnp.exp(-jnp.abs(z))
    log_sig = jnp.minimum(z, 0.0) - jnp.log1p(ez)
    t = log_1m_lb + log_sig
    mx = jnp.maximum(log_lb, t)
    hlf_ref[...] = mx + jnp.log1p(jnp.exp(-jnp.abs(log_lb - t)))
    hk_ref[...] = (1.0 - lb) * (jnp.where(z >= 0.0, ez, 1.0) / (1.0 + ez))

    hv_ref[...] = proj(c0 + 2 * hw, hvw).astype(BF16)
    hg_ref[...] = proj(c0 + 2 * hw + hvw, hvw)

    zf = proj(c0 + 2 * hw + 2 * hvw, V7X_LANES) + fb_ref[...]
    lf_ref[...] = jnp.minimum(zf, 0.0) - jnp.log1p(jnp.exp(-jnp.abs(zf)))


def _inproj(x, gain, w_mix, gsum, q_gain, k_gain, f_bias, lb_logits, *, fw, hw, hvw, dh, layer):
    n, d = x.shape
    tm = min(ROW_TILE, n)
    assert n % tm == 0
    row = lambda w: pl.BlockSpec((tm, w), lambda i: (i, 0))
    sds = lambda w, dt: jax.ShapeDtypeStruct((n, w), dt)
    return pl.pallas_call(
        functools.partial(_inproj_body, fw=fw, hw=hw, hvw=hvw, dh=dh, layer=layer),
        grid=(n // tm,),
        in_specs=[row(d), _resident((1, d)), _resident(w_mix.shape), _resident(gsum.shape),
                  _resident((1, fw)), _resident((1, fw)), _resident((1, V7X_LANES)),
                  _resident(lb_logits.shape)],
        out_specs=[row(fw), row(fw), row(fw), row(fw), row(fw), row(V7X_LANES),
                   row(hw), row(hw), row(hw), row(hvw), row(hvw)],
        out_shape=[sds(fw, BF16), sds(fw, F32), sds(fw, F32), sds(fw, BF16), sds(fw, BF16),
                   sds(V7X_LANES, F32),
                   sds(hw, F32), sds(hw, F32), sds(hw, F32), sds(hvw, BF16), sds(hvw, F32)],
        compiler_params=_params("parallel"),
        name="inproj",
    )(x, gain, w_mix, gsum, q_gain, k_gain, f_bias, lb_logits)


def _cumsum_body(lf_ref, tril_ref, c_ref, carry_ref):
    @pl.when(pl.program_id(1) == 0)
    def _():
        carry_ref[...] = jnp.zeros_like(carry_ref)

    c = _dot01(tril_ref[...], lf_ref[...]) + carry_ref[...]
    c_ref[...] = c
    carry_ref[...] = c[-1:, :]


def _cumsum(lf, tril):
    b, l, w = lf.shape
    tc = tril.shape[0]
    return pl.pallas_call(
        _cumsum_body,
        grid=(b, l // tc),
        in_specs=[pl.BlockSpec((None, tc, w), lambda i, j: (i, j, 0)), _resident(tril.shape)],
        out_specs=pl.BlockSpec((None, tc, w), lambda i, j: (i, j, 0)),
        out_shape=jax.ShapeDtypeStruct((b, l, w), F32),
        scratch_shapes=[pltpu.VMEM((1, w), F32)],
        compiler_params=_params("parallel", "arbitrary"),
        name="logf_cumsum",
    )(lf, tril)


def _attn_body(q_ref, k_ref, v_ref, crow_ref, ccol_ref, o_ref, m_ref, l_ref, acc_ref, *, t, dh):
    p = pl.program_id(1)
    qi = pl.program_id(2)
    q = q_ref[...]
    lane = lax.broadcasted_iota(jnp.int32, (1, V7X_LANES), 1)
    row = lax.broadcasted_iota(jnp.int32, (t, t), 0)
    col = lax.broadcasted_iota(jnp.int32, (t, t), 1)
    outs = []
    for hh in range(V7X_LANES // dh):
        in_head = (lane // dh) == hh
        qm = jnp.where(in_head, q, jnp.zeros_like(q))
        head = p * (V7X_LANES // dh) + hh
        ccol = jnp.sum(jnp.where(lane == head, ccol_ref[...], 0.0), axis=-1, keepdims=True)
        m_ref[...] = jnp.full_like(m_ref, NEG_INF)
        l_ref[...] = jnp.zeros_like(l_ref)
        acc_ref[...] = jnp.zeros_like(acc_ref)

        def step(kb, masked):
            r0 = pl.multiple_of(kb * t, t)
            k = k_ref[pl.ds(r0, t), :]
            v = v_ref[pl.ds(r0, t), :]
            u = _dot_nt(qm, k) - crow_ref[pl.ds(head, 1), pl.ds(r0, t)]
            if masked:
                u = jnp.where(col <= row, u, NEG_INF)
            m_old = m_ref[...]
            m_new = jnp.maximum(m_old, jnp.max(u, axis=-1, keepdims=True) + ccol)
            pe = jnp.exp(u - (m_new - ccol))
            alpha = jnp.exp(m_old - m_new)
            l_ref[...] = alpha * l_ref[...] + jnp.sum(pe, axis=-1, keepdims=True)
            acc_ref[...] = alpha * acc_ref[...] + _dot(pe.astype(BF16), v)
            m_ref[...] = m_new

        def body(kb, carry):
            step(kb, False)
            return carry

        lax.fori_loop(0, qi, body, 0)
        step(qi, True)
        outs.append(acc_ref[...] / l_ref[...])
    o = outs[0]
    for hh in range(1, len(outs)):
        o = jnp.where((lane // dh) == hh, outs[hh], o)
    o_ref[...] = o.astype(o_ref.dtype)


def _attn_prompt(q, k, v, crow, ccol, *, b, l, dh):
    n, fw = q.shape
    t = min(ATTN_TILE, l)
    nq = l // t
    assert l % t == 0 and fw % V7X_LANES == 0 and V7X_LANES % dh == 0
    return pl.pallas_call(
        functools.partial(_attn_body, t=t, dh=dh),
        grid=(b, fw // V7X_LANES, nq),
        in_specs=[pl.BlockSpec((t, V7X_LANES), lambda i, p, j: (i * nq + j, p)),
                  pl.BlockSpec((l, V7X_LANES), lambda i, p, j: (i, p)),
                  pl.BlockSpec((l, V7X_LANES), lambda i, p, j: (i, p)),
                  pl.BlockSpec((None,) + crow.shape[1:], lambda i, p, j: (i, 0, 0)),
                  pl.BlockSpec((None, t, V7X_LANES), lambda i, p, j: (i, j, 0))],
        out_specs=pl.BlockSpec((t, V7X_LANES), lambda i, p, j: (i * nq + j, p)),
        out_shape=jax.ShapeDtypeStruct((n, fw), BF16),
        scratch_shapes=[pltpu.VMEM((t, 1), F32), pltpu.VMEM((t, 1), F32), pltpu.VMEM((t, V7X_LANES), F32)],
        compiler_params=_params("parallel", "parallel", "arbitrary"),
        name="fox_prompt",
    )(q, k, v, crow, ccol)


def _gla_masks(c):
    r = np.arange(c)[:, None]
    s = np.arange(c)[None, :]
    tril = (s <= r).astype(np.float32)
    levels = []
    m = c // 2
    while m >= GLA_LEAF:
        levels.append((((r ^ s) < 2 * m) & ((r & m) != 0) & ((s & m) == 0)).astype(np.float32))
        m //= 2
    if not levels:
        levels.append(np.zeros((c, c), np.float32))
    return jnp.asarray(tril, BF16), jnp.asarray(np.stack(levels), F32)


def _gla_chunk(q, k, lf, v, st, tril, lvl_ref, *, c, kd):
    b = _dot01(tril, lf)
    q = q * (kd ** -0.5)
    b_last = b[c - 1:c, :]
    o = _dot_nt((q * jnp.exp(b)).astype(BF16), st.astype(BF16))
    k_hat = (k * jnp.exp(b_last - b)).astype(BF16)
    st_new = st * jnp.exp(b_last) + _dot_tn(v, k_hat)

    m = c // 2
    lvl = 0
    scores = None
    while m >= GLA_LEAF:
        pieces = []
        for blk in range(c // (2 * m)):
            mid = blk * 2 * m + m
            pieces.append(jnp.broadcast_to(b[mid - 1:mid, :], (2 * m, kd)))
        ref = pieces[0] if len(pieces) == 1 else jnp.concatenate(pieces, axis=0)
        e = jnp.exp(-jnp.abs(b - ref))
        part = lvl_ref[lvl] * _dot_nt((q * e).astype(BF16), (k * e).astype(BF16))
        scores = part if scores is None else scores + part
        m //= 2
        lvl += 1
    if scores is not None:
        o = o + _dot(scores.astype(BF16), v)

    nl = c // GLA_LEAF
    b3 = b.reshape(nl, GLA_LEAF, kd)
    q3 = q.reshape(nl, GLA_LEAF, kd)
    k3 = k.reshape(nl, GLA_LEAF, kd)
    v3 = v.astype(F32).reshape(nl, GLA_LEAF, v.shape[-1])
    t_idx = lax.broadcasted_iota(jnp.int32, (nl, GLA_LEAF, kd), 1)
    o3 = jnp.zeros(v3.shape, F32)
    for s in range(GLA_LEAF):
        decay = jnp.exp(jnp.where(t_idx >= s, b3 - b3[:, s:s + 1, :], NEG_INF))
        w = jnp.sum(decay * q3 * k3[:, s:s + 1, :], axis=-1, keepdims=True)
        o3 = o3 + w * v3[:, s:s + 1, :]
    o = o + o3.reshape(c, v.shape[-1])
    return o, st_new


def _gla_body(q_ref, k_ref, lf_ref, v_ref, s0_ref, tril_ref, lvl_ref, o_ref, sT_ref, st_ref,
              *, c, rows, nh, kd, vd):
    j = pl.program_id(1)

    @pl.when(j == 0)
    def _():
        st_ref[...] = s0_ref[...]

    tril = tril_ref[...]

    def chunk(ci, carry):
        r0 = pl.multiple_of(ci * c, c)
        rs = pl.ds(r0, c)
        for h in range(nh):
            ks = slice(h * kd, (h + 1) * kd)
            vs = slice(h * vd, (h + 1) * vd)
            o, st_new = _gla_chunk(q_ref[rs, ks], k_ref[rs, ks], lf_ref[rs, ks], v_ref[rs, vs],
                                   st_ref[h], tril, lvl_ref, c=c, kd=kd)
            o_ref[rs, vs] = o
            st_ref[h] = st_new
        return carry

    if rows == c:
        chunk(0, 0)
    else:
        lax.fori_loop(0, rows // c, chunk, 0)

    @pl.when(j == pl.num_programs(1) - 1)
    def _():
        sT_ref[...] = st_ref[...]


def _gla(q, k, lf, v, s0_t, *, b, l):
    n, hw = q.shape
    hvw = v.shape[1]
    nh, vd, kd = s0_t.shape[1:]
    c = min(GLA_CHUNK, l)
    rows = min(GLA_ROWS, l)
    assert l % rows == 0 and rows % c == 0 and c % GLA_LEAF == 0
    tril, lvls = _gla_masks(c)
    nr = l // rows
    row = lambda w: pl.BlockSpec((rows, w), lambda i, j: (i * nr + j, 0))
    state = pl.BlockSpec((None, nh, vd, kd), lambda i, j: (i, 0, 0, 0))
    return pl.pallas_call(
        functools.partial(_gla_body, c=c, rows=rows, nh=nh, kd=kd, vd=vd),
        grid=(b, nr),
        in_specs=[row(hw), row(hw), row(hw), row(hvw), state, _resident(tril.shape), _resident(lvls.shape)],
        out_specs=[row(hvw), state],
        out_shape=[jax.ShapeDtypeStruct((n, hvw), F32), jax.ShapeDtypeStruct(s0_t.shape, F32)],
        scratch_shapes=[pltpu.VMEM((nh, vd, kd), F32)],
        compiler_params=_params("parallel", "arbitrary"),
        name="hgrn2",
    )(q, k, lf, v, s0_t, tril, lvls)


def _merge_body(x_ref, g_ref, wg_ref, of_ref, oh_ref, hg_ref, og_ref, gsum_ref, wpf_ref, wph_ref, wo_ref,
                o_ref, *, d, vd):
    x = x_ref[...]
    h = _rms_rows(x, g_ref[...]).astype(BF16)
    gate_fox = _sigmoid(_dot(h, wg_ref[:, :d]))
    gate_hg = _sigmoid(_dot(h, wg_ref[:, d:]))
    y_fox = _dot(of_ref[...], wpf_ref[...])
    oh = oh_ref[...]
    ms = _dot((oh * oh).astype(BF16), gsum_ref[...]) * (1.0 / vd)
    hg = hg_ref[...]
    ob = (oh * lax.rsqrt(ms + EPS) * og_ref[...]) * (hg * _sigmoid(hg))
    y_hg = _dot(ob.astype(BF16), wph_ref[...])
    y = gate_fox * y_fox + gate_hg * y_hg
    o_ref[...] = x + _dot(y.astype(BF16), wo_ref[...])


def _merge(x, gain, w_gate, o_fox, o_hg, hgate, o_gain, gsum, w_pf, w_ph, w_out, *, vd):
    n, d = x.shape
    tm = min(ROW_TILE, n)
    assert n % tm == 0
    row = lambda w: pl.BlockSpec((tm, w), lambda i: (i, 0))
    return pl.pallas_call(
        functools.partial(_merge_body, d=d, vd=vd),
        grid=(n // tm,),
        in_specs=[row(d), _resident((1, d)), _resident(w_gate.shape), row(o_fox.shape[1]), row(o_hg.shape[1]),
                  row(hgate.shape[1]), _resident(o_gain.shape), _resident(gsum.shape),
                  _resident(w_pf.shape), _resident(w_ph.shape), _resident(w_out.shape)],
        out_specs=row(d),
        out_shape=jax.ShapeDtypeStruct((n, d), F32),
        compiler_params=_params("parallel"),
        name="merge",
    )(x, gain, w_gate, o_fox, o_hg, hgate, o_gain, gsum, w_pf, w_ph, w_out)


def _suffix_body(pt_ref, *refs, g, ps):
    lp_refs = refs[:g]
    w_ref, out_ref, carry_ref = refs[g:]

    @pl.when(pl.program_id(1) == 0)
    def _():
        carry_ref[...] = jnp.zeros_like(carry_ref)

    w = w_ref[...]
    carry = carry_ref[...]
    for i in reversed(range(g)):
        r = _dot01_tn(lp_refs[i][...], w)
        out_ref[:, i * ps:(i + 1) * ps] = r[:, :ps] + carry
        carry = carry + r[:, ps:]
    carry_ref[...] = carry


def _suffix(cache_logf, page_table, layer):
    _, _, ps, nh = cache_logf.shape
    nb, n_pages = page_table.shape
    g = min(SUFFIX_PAGES_PER_STEP, n_pages)
    ng = n_pages // g
    assert n_pages % g == 0
    j_idx = np.arange(ps)
    w = np.concatenate([(j_idx[:, None] > j_idx[None, :]), np.ones((ps, ps), bool)], axis=1)
    w = jnp.asarray(w.astype(np.float32), BF16)

    def page(i):
        return pl.BlockSpec((None, None, ps, nh), lambda b, j, pt: (layer, pt[b, (ng - 1 - j) * g + i], 0, 0))

    grid_spec = pltpu.PrefetchScalarGridSpec(
        num_scalar_prefetch=1,
        grid=(nb, ng),
        in_specs=[page(i) for i in range(g)] + [pl.BlockSpec(w.shape, lambda b, j, pt: (0, 0))],
        out_specs=pl.BlockSpec((None, nh, g * ps), lambda b, j, pt: (b, 0, ng - 1 - j)),
        scratch_shapes=[pltpu.VMEM((nh, ps), F32)],
    )
    return pl.pallas_call(
        functools.partial(_suffix_body, g=g, ps=ps),
        grid_spec=grid_spec,
        out_shape=jax.ShapeDtypeStruct((nb, nh, n_pages * ps), F32),
        compiler_params=_params("parallel", "arbitrary"),
        name="logf_suffix",
    )(page_table, *([cache_logf] * g), w)


def _paged_body(pt_ref, *refs, g, ps, nh, dh, t):
    k_refs = refs[:g]
    v_refs = refs[g:2 * g]
    (suf_ref, q_ref, kn_ref, vn_ref, lfn_ref, o_ref, qbd_ref, ctcol_ref, m_ref, l_ref, acc_ref) = refs[2 * g:]
    j = pl.program_id(1)
    rows = nh * t
    fw = nh * dh
    row_head = lax.broadcasted_iota(jnp.int32, (rows, 1), 0) // t

    def new_token_cumsum():
        lfn = lfn_ref[...]
        sub = lax.broadcasted_iota(jnp.int32, lfn.shape, 0)
        ct = jnp.zeros_like(lfn)
        for s in range(t):
            ct = ct + jnp.where(sub >= s, lfn[s:s + 1, :], 0.0)
        return ct

    @pl.when(j == 0)
    def _():
        lane_head = lax.broadcasted_iota(jnp.int32, (rows, fw), 1) // dh
        q_rep = jnp.concatenate([q_ref[...]] * nh, axis=0)
        qbd_ref[...] = jnp.where(lane_head == row_head, q_rep, jnp.zeros_like(q_rep))
        ct_rep = jnp.concatenate([new_token_cumsum()] * nh, axis=0)
        lane = lax.broadcasted_iota(jnp.int32, ct_rep.shape, 1)
        ctcol_ref[...] = jnp.sum(jnp.where(lane == row_head, ct_rep, 0.0), axis=-1, keepdims=True)
        m_ref[...] = jnp.full_like(m_ref, NEG_INF)
        l_ref[...] = jnp.zeros_like(l_ref)
        acc_ref[...] = jnp.zeros_like(acc_ref)

    qbd = qbd_ref[...]
    ctcol = ctcol_ref[...]

    def update(u, pv_fn):
        m_old = m_ref[...]
        m_new = jnp.maximum(m_old, jnp.max(u, axis=-1, keepdims=True) + ctcol)
        pe = jnp.exp(u - (m_new - ctcol))
        alpha = jnp.exp(m_old - m_new)
        l_ref[...] = alpha * l_ref[...] + jnp.sum(pe, axis=-1, keepdims=True)
        acc_ref[...] = alpha * acc_ref[...] + pv_fn(pe)
        m_ref[...] = m_new

    s = jnp.concatenate([_dot_nt(qbd, k_refs[i][...].astype(BF16)) for i in range(g)], axis=1)
    suf = suf_ref[...]
    bias = jnp.broadcast_to(suf[:, None, :], (nh, t, g * ps)).reshape(rows, g * ps)

    def pv_past(pe):
        out = None
        for i in range(g):
            part = _dot(pe[:, i * ps:(i + 1) * ps].astype(BF16), v_refs[i][...].astype(BF16))
            out = part if out is None else out + part
        return out

    update(s + bias, pv_past)

    @pl.when(j == pl.num_programs(1) - 1)
    def _():
        pad = jnp.zeros((ps - t, fw), BF16)
        kn = jnp.concatenate([kn_ref[...], pad], axis=0)
        vn = jnp.concatenate([vn_ref[...], pad], axis=0)
        ct = new_token_cumsum()
        ct_pad = jnp.concatenate([ct, jnp.zeros((ps - t, ct.shape[1]), F32)], axis=0)
        lane = lax.broadcasted_iota(jnp.int32, (rows, ct.shape[1]), 1)
        onehot = jnp.where(lane == row_head, 1.0, 0.0).astype(BF16)
        c_s = _dot01_nt(onehot, ct_pad)
        col = lax.broadcasted_iota(jnp.int32, (rows, ps), 1)
        row_t = lax.broadcasted_iota(jnp.int32, (rows, ps), 0) % t
        u = jnp.where(col <= row_t, _dot_nt(qbd, kn) - c_s, NEG_INF)
        update(u, lambda pe: _dot(pe.astype(BF16), vn))
        o = acc_ref[...] / l_ref[...]
        lane_head = lax.broadcasted_iota(jnp.int32, (rows, fw), 1) // dh
        o = jnp.where(lane_head == row_head, o, 0.0)
        o_ref[...] = jnp.sum(o.reshape(nh, t, fw), axis=0).astype(o_ref.dtype)


def _attn_paged(cache_k, cache_v, page_table, suffix, q, k_new, v_new, lf_new, *, layer, t, dh):
    _, _, ps, fw = cache_k.shape
    nb, n_pages = page_table.shape
    nh = fw // dh
    g = min(PAGES_PER_STEP, n_pages)
    ng = n_pages // g
    rows = nh * t
    assert n_pages % g == 0 and t <= ps

    def page(i):
        return pl.BlockSpec((None, None, ps, fw), lambda b, j, pt: (layer, pt[b, j * g + i], 0, 0))

    new_rows = lambda w: pl.BlockSpec((t, w), lambda b, j, pt: (b, 0))
    grid_spec = pltpu.PrefetchScalarGridSpec(
        num_scalar_prefetch=1,
        grid=(nb, ng),
        in_specs=[page(i) for i in range(g)] + [page(i) for i in range(g)] + [
            pl.BlockSpec((None, nh, g * ps), lambda b, j, pt: (b, 0, j)),
            new_rows(fw), new_rows(fw), new_rows(fw), new_rows(lf_new.shape[1])],
        out_specs=new_rows(fw),
        scratch_shapes=[pltpu.VMEM((rows, fw), BF16), pltpu.VMEM((rows, 1), F32), pltpu.VMEM((rows, 1), F32),
                        pltpu.VMEM((rows, 1), F32), pltpu.VMEM((rows, fw), F32)],
    )
    return pl.pallas_call(
        functools.partial(_paged_body, g=g, ps=ps, nh=nh, dh=dh, t=t),
        grid_spec=grid_spec,
        out_shape=jax.ShapeDtypeStruct((nb * t, fw), BF16),
        compiler_params=_params("parallel", "arbitrary"),
        name="fox_paged",
    )(page_table, *([cache_k] * g), *([cache_v] * g), suffix, q, k_new, v_new, lf_new)


def _block_ones(width, group):
    i = np.arange(width) // group
    return jnp.asarray((i[:, None] == i[None, :]).astype(np.float32), BF16)


def kernel(x_prompt, x_sample, cache_k, cache_v, cache_logf, state_hgrn, page_table, ffn1_norm, ffn1_w_in, ffn1_w_out, mix_norm, w_in, fox_f_bias, fox_q_gain, fox_k_gain, hg_lb_logits, hg_o_gain, w_proj_fox, w_proj_hg, w_out, ffn2_norm, ffn2_w_in, ffn2_w_out):
    depth = w_in.shape[0]
    bp, lp, d = x_prompt.shape
    bs, ls, _ = x_sample.shape
    nh, dh = cache_k.shape[-2:]
    fw = nh * dh
    hg_heads, kd, vd = state_hgrn.shape[-3:]
    hw, hvw = hg_heads * kd, hg_heads * vd
    assert nh <= V7X_LANES and w_in.shape[-1] == 3 * fw + nh + 2 * hw + 2 * hvw + 2 * d

    gsum_fox = _block_ones(fw, dh)
    gsum_hg = _block_ones(hvw, vd)
    tc = min(ROW_TILE, lp)
    tril_seq = jnp.asarray(np.tril(np.ones((tc, tc), np.float32)), BF16)
    lb_logits = hg_lb_logits.astype(F32)
    cache_k4 = cache_k.reshape(cache_k.shape[:3] + (fw,))
    cache_v4 = cache_v.reshape(cache_v.shape[:3] + (fw,))

    yp = x_prompt.reshape(bp * lp, d)
    ys = x_sample.reshape(bs * ls, d)
    outs = {name: [] for name in ("kp", "vp", "fp", "sp", "ks", "vs", "fs", "ss")}

    for l in range(depth):
        c_mix = 3 * fw + nh
        w_l = w_in[l]
        w_mix = jnp.concatenate(
            [w_l[:, :3 * fw], w_l[:, c_mix:c_mix + 2 * hw + 2 * hvw],
             jnp.pad(w_l[:, 3 * fw:c_mix], ((0, 0), (0, V7X_LANES - nh)))], axis=1).astype(BF16)
        w_gate = w_l[:, c_mix + 2 * hw + 2 * hvw:].astype(BF16)
        f_bias = jnp.pad(fox_f_bias[l], (0, V7X_LANES - nh)).reshape(1, V7X_LANES)
        q_gain = jnp.tile(fox_q_gain[l], nh).reshape(1, fw)
        k_gain = jnp.tile(fox_k_gain[l], nh).reshape(1, fw)
        o_gain = jnp.tile(hg_o_gain[l], hg_heads).reshape(1, hvw)
        mix_gain = mix_norm[l].reshape(1, d)
        w1_in, w1_out = ffn1_w_in[l].astype(BF16), ffn1_w_out[l].astype(BF16)
        w2_in, w2_out = ffn2_w_in[l].astype(BF16), ffn2_w_out[l].astype(BF16)
        w_pf, w_ph, w_o = w_proj_fox[l].astype(BF16), w_proj_hg[l].astype(BF16), w_out[l].astype(BF16)

        yp = _ffn(yp, ffn1_norm[l].reshape(1, d), w1_in, w1_out)
        ys = _ffn(ys, ffn1_norm[l].reshape(1, d), w1_in, w1_out)

        inproj = functools.partial(_inproj, gain=mix_gain, w_mix=w_mix, gsum=gsum_fox, q_gain=q_gain,
                                   k_gain=k_gain, f_bias=f_bias, lb_logits=lb_logits,
                                   fw=fw, hw=hw, hvw=hvw, dh=dh, layer=l)
        merge = functools.partial(_merge, gain=mix_gain, w_gate=w_gate, o_gain=o_gain, gsum=gsum_hg,
                                  w_pf=w_pf, w_ph=w_ph, w_out=w_o, vd=vd)

        q, kf, vf, kb, vb, lf, hq, hk, hlf, hv, hgate = inproj(yp)
        c = _cumsum(lf.reshape(bp, lp, V7X_LANES), tril_seq)
        c_row = jnp.transpose(c[:, :, :nh], (0, 2, 1))
        o_fox = _attn_prompt(q, kb, vb, c_row, c, b=bp, l=lp, dh=dh)
        s0_t = jnp.zeros((bp, hg_heads, vd, kd), F32)
        o_hg, s_t = _gla(hq, hk, hlf, hv, s0_t, b=bp, l=lp)
        yp = merge(yp, o_fox=o_fox, o_hg=o_hg, hgate=hgate)
        outs["kp"].append(kf.reshape(bp, lp, nh, dh))
        outs["vp"].append(vf.reshape(bp, lp, nh, dh))
        outs["fp"].append(lf[:, :nh].reshape(bp, lp, nh).astype(cache_logf.dtype))
        outs["sp"].append(jnp.swapaxes(s_t, -1, -2).astype(x_prompt.dtype))

        q, kf, vf, kb, vb, lf, hq, hk, hlf, hv, hgate = inproj(ys)
        suffix = _suffix(cache_logf, page_table, l)
        o_fox = _attn_paged(cache_k4, cache_v4, page_table, suffix, q, kb, vb, lf, layer=l, t=ls, dh=dh)
        s0_t = jnp.swapaxes(state_hgrn[l].astype(F32), -1, -2)
        o_hg, s_t = _gla(hq, hk, hlf, hv, s0_t, b=bs, l=ls)
        ys = merge(ys, o_fox=o_fox, o_hg=o_hg, hgate=hgate)
        outs["ks"].append(kf.reshape(bs, ls, nh, dh))
        outs["vs"].append(vf.reshape(bs, ls, nh, dh))
        outs["fs"].append(lf[:, :nh].reshape(bs, ls, nh).astype(cache_logf.dtype))
        outs["ss"].append(jnp.swapaxes(s_t, -1, -2).astype(state_hgrn.dtype))

        yp = _ffn(yp, ffn2_norm[l].reshape(1, d), w2_in, w2_out)
        ys = _ffn(ys, ffn2_norm[l].reshape(1, d), w2_in, w2_out)

    st = lambda name: jnp.stack(outs[name])
    return (yp.reshape(bp, lp, d), ys.reshape(bs, ls, d), st("kp"), st("vp"), st("fp"), st("sp"),
            st("ks"), st("vs"), st("fs"), st("ss"))
```

```python
import functools

import jax
import jax.numpy as jnp
import numpy as np
from jax import lax
from jax.experimental import pallas as pl
from jax.experimental.pallas import tpu as pltpu

F32 = jnp.float32
BF16 = jnp.bfloat16
EPS = 1e-6
NEG_INF = float("-inf")
LOG2E = 1.4426950408889634

V7X_LANES = 128
V7X_SUBLANES = 8
V7X_VMEM_BYTES = 64 * 1024 * 1024
VMEM_LIMIT = (V7X_VMEM_BYTES * 3) // 4

ROW_TILE = 512
ATTN_TILE = 512
GLA_CHUNK = 64
GLA_LEAF = V7X_SUBLANES
GLA_ROWS = 512
PAGES_PER_STEP = 16


def _params(*sem):
    return pltpu.CompilerParams(dimension_semantics=sem, vmem_limit_bytes=VMEM_LIMIT)


def _resident(shape):
    nd = len(shape)
    return pl.BlockSpec(shape, lambda *_: (0,) * nd, pipeline_mode=pl.Buffered(1))


def _dot(a, b):
    return jnp.dot(a, b, preferred_element_type=F32)


def _dot_nt(a, b):
    return lax.dot_general(a, b, (((1,), (1,)), ((), ())), preferred_element_type=F32)


def _dot_tn(a, b):
    return lax.dot_general(a, b, (((0,), (0,)), ((), ())), preferred_element_type=F32)


def _split3(x):
    hi = x.astype(BF16)
    r1 = x - hi.astype(F32)
    mid = r1.astype(BF16)
    lo = (r1 - mid.astype(F32)).astype(BF16)
    return hi, mid, lo


def _dot01(m01, x):
    hi, mid, lo = _split3(x)
    return _dot(m01, hi) + _dot(m01, mid) + _dot(m01, lo)


def _dot01_nt(m01, x):
    hi, mid, lo = _split3(x)
    return _dot_nt(m01, hi) + _dot_nt(m01, mid) + _dot_nt(m01, lo)


def _dot01_r(x, m01):
    hi, mid, lo = _split3(x)
    return _dot(hi, m01) + _dot(mid, m01) + _dot(lo, m01)


def _rms_rows(x, gain):
    return x * lax.rsqrt(jnp.mean(x * x, axis=-1, keepdims=True) + EPS) * gain


def _sigmoid(x):
    return 1.0 / (1.0 + jnp.exp(-x))


def _ffn_body(x_ref, g_ref, wi_ref, wo_ref, o_ref, acc_ref, *, d_ff, tf):
    x = x_ref[...]
    h = _rms_rows(x, g_ref[...]).astype(BF16)
    for j in range(d_ff // tf):
        a = _dot(h, wi_ref[:, j * tf:(j + 1) * tf])
        b = _dot(h, wi_ref[:, d_ff + j * tf:d_ff + (j + 1) * tf])
        g = (a * _sigmoid(a) * b).astype(BF16)
        y = _dot(g, wo_ref[j * tf:(j + 1) * tf, :])
        if j == 0:
            acc_ref[...] = y
        else:
            acc_ref[...] += y
    o_ref[...] = x + 0.5 * acc_ref[...]


def _ffn(x, gain, w_in, w_out):
    n, d = x.shape
    d_ff = w_out.shape[0]
    tm = min(ROW_TILE, n)
    tf = 2 * V7X_LANES if d_ff % (2 * V7X_LANES) == 0 else V7X_LANES
    assert n % tm == 0 and d_ff % tf == 0
    return pl.pallas_call(
        functools.partial(_ffn_body, d_ff=d_ff, tf=tf),
        grid=(n // tm,),
        in_specs=[pl.BlockSpec((tm, d), lambda i: (i, 0)),
                  _resident((1, d)),
                  _resident((d, 2 * d_ff)),
                  _resident((d_ff, d))],
        out_specs=pl.BlockSpec((tm, d), lambda i: (i, 0)),
        out_shape=jax.ShapeDtypeStruct((n, d), F32),
        scratch_shapes=[pltpu.VMEM((tm, d), F32)],
        compiler_params=_params("parallel"),
        name="ffn",
    )(x, gain, w_in, w_out)


def _inproj_body(x_ref, g_ref, w_ref, gsum_ref, qg_ref, kg_ref, fb_ref, lbl_ref,
                 q_ref, kf_ref, vf_ref, kb_ref, vb_ref, lf_ref,
                 hq_ref, hk_ref, hlf_ref, hv_ref, hg_ref, *, fw, hw, hvw, dh, layer):
    h = _rms_rows(x_ref[...], g_ref[...]).astype(BF16)

    def proj(c0, n):
        return _dot(h, w_ref[:, c0:c0 + n])

    gsum = gsum_ref[...]

    def head_norm(z, gain):
        ms = _dot((z * z).astype(BF16), gsum) * (1.0 / dh)
        return z * lax.rsqrt(ms + EPS) * gain

    q = head_norm(proj(0, fw), qg_ref[...])
    q_ref[...] = (q * (dh ** -0.5 * LOG2E)).astype(BF16)
    k = head_norm(proj(fw, fw), kg_ref[...])
    kf_ref[...] = k
    kb_ref[...] = k.astype(BF16)
    v = proj(2 * fw, fw)
    vf_ref[...] = v
    vb_ref[...] = v.astype(BF16)

    c0 = 3 * fw
    hq = proj(c0, hw)
    hq_ref[...] = hq * _sigmoid(hq)

    lg = lbl_ref[...]
    e = jnp.exp(lg - jnp.max(lg, axis=0, keepdims=True))
    sm = e / jnp.sum(e, axis=0, keepdims=True)
    lb = jnp.zeros_like(sm[0:1])
    for i in range(1, layer + 1):
        lb = lb + sm[i:i + 1]
    log_lb = jnp.log(lb)
    log_1m_lb = jnp.log1p(-lb)

    z = proj(c0 + hw, hw)
    ez = jnp.exp(-jnp.abs(z))
    log_sig = jnp.minimum(z, 0.0) - jnp.log1p(ez)
    t = log_1m_lb + log_sig
    mx = jnp.maximum(log_lb, t)
    hlf_ref[...] = mx + jnp.log1p(jnp.exp(-jnp.abs(log_lb - t)))
    hk_ref[...] = (1.0 - lb) * (jnp.where(z >= 0.0, ez, 1.0) / (1.0 + ez))

    hv_ref[...] = proj(c0 + 2 * hw, hvw).astype(BF16)
    hg_ref[...] = proj(c0 + 2 * hw + hvw, hvw)

    zf = proj(c0 + 2 * hw + 2 * hvw, V7X_LANES) + fb_ref[...]
    lf_ref[...] = jnp.minimum(zf, 0.0) - jnp.log1p(jnp.exp(-jnp.abs(zf)))


def _inproj(x, gain, w_mix, gsum, q_gain, k_gain, f_bias, lb_logits, *, fw, hw, hvw, dh, layer):
    n, d = x.shape
    tm = min(ROW_TILE, n)
    assert n % tm == 0
    row = lambda w: pl.BlockSpec((tm, w), lambda i: (i, 0))
    sds = lambda w, dt: jax.ShapeDtypeStruct((n, w), dt)
    return pl.pallas_call(
        functools.partial(_inproj_body, fw=fw, hw=hw, hvw=hvw, dh=dh, layer=layer),
        grid=(n // tm,),
        in_specs=[row(d), _resident((1, d)), _resident(w_mix.shape), _resident(gsum.shape),
                  _resident((1, fw)), _resident((1, fw)), _resident((1, V7X_LANES)),
                  _resident(lb_logits.shape)],
        out_specs=[row(fw), row(fw), row(fw), row(fw), row(fw), row(V7X_LANES),
                   row(hw), row(hw), row(hw), row(hvw), row(hvw)],
        out_shape=[sds(fw, BF16), sds(fw, F32), sds(fw, F32), sds(fw, BF16), sds(fw, BF16),
                   sds(V7X_LANES, F32),
                   sds(hw, F32), sds(hw, F32), sds(hw, F32), sds(hvw, BF16), sds(hvw, F32)],
        compiler_params=_params("parallel"),
        name="inproj",
    )(x, gain, w_mix, gsum, q_gain, k_gain, f_bias, lb_logits)


BIAS_PIECES = 3


def _bias_lane(hh, dh):
    return ((hh + 1) % (V7X_LANES // dh)) * dh


def _cumsum_body(lf_ref, k_ref, tril_ref, place_ref, c_ref, kaug_ref, carry_ref, *, nh, dh):
    @pl.when(pl.program_id(1) == 0)
    def _():
        carry_ref[...] = jnp.zeros_like(carry_ref)

    c = _dot01(tril_ref[...], lf_ref[...]) + carry_ref[...]
    c_ref[...] = c
    carry_ref[...] = c[-1:, :]
    pieces = jnp.concatenate(_split3(-LOG2E * c), axis=1)
    lane = lax.broadcasted_iota(jnp.int32, (1, V7X_LANES), 1)
    hp = V7X_LANES // dh
    for h in range(nh):
        tile = h // hp
        placed = _dot(pieces, place_ref[h]).astype(BF16)
        keys = k_ref[:, tile * V7X_LANES:(tile + 1) * V7X_LANES]
        kaug_ref[h] = jnp.where((lane // dh) == (h % hp), keys, placed)


def _cumsum(lf, k, tril, *, nh, dh):
    b, l, w = lf.shape
    tc = tril.shape[0]
    nt = l // tc
    place = np.zeros((nh, BIAS_PIECES * w, V7X_LANES), np.float32)
    for h in range(nh):
        for piece in range(BIAS_PIECES):
            place[h, piece * w + h, _bias_lane(h % (V7X_LANES // dh), dh) + piece] = 1.0
    place = jnp.asarray(place, BF16)
    return pl.pallas_call(
        functools.partial(_cumsum_body, nh=nh, dh=dh),
        grid=(b, nt),
        in_specs=[pl.BlockSpec((None, tc, w), lambda i, j: (i, j, 0)),
                  pl.BlockSpec((tc, k.shape[1]), lambda i, j: (i * nt + j, 0)),
                  _resident(tril.shape), _resident(place.shape)],
        out_specs=[pl.BlockSpec((None, tc, w), lambda i, j: (i, j, 0)),
                   pl.BlockSpec((None, nh, tc, V7X_LANES), lambda i, j: (i, 0, j, 0))],
        out_shape=[jax.ShapeDtypeStruct((b, l, w), F32), jax.ShapeDtypeStruct((b, nh, l, V7X_LANES), BF16)],
        scratch_shapes=[pltpu.VMEM((1, w), F32)],
        compiler_params=_params("parallel", "arbitrary"),
        name="logf_cumsum",
    )(lf, k, tril, place)


def _attn_body(q_ref, kaug_ref, vt_ref, crow_ref, o_ref, m_ref, acc_ref, *, t, dh):
    p = pl.program_id(1)
    qi = pl.program_id(2)
    hp = V7X_LANES // dh
    q = q_ref[...]
    lane = lax.broadcasted_iota(jnp.int32, (1, V7X_LANES), 1)
    key_i = lax.broadcasted_iota(jnp.int32, (t, t), 0)
    qry_i = lax.broadcasted_iota(jnp.int32, (t, t), 1)
    feat = lax.broadcasted_iota(jnp.int32, (V7X_LANES, t), 0)
    qas, cts, keep, ones = [], [], [], []
    for hh in range(hp):
        b0 = _bias_lane(hh, dh)
        one_lanes = jnp.where((lane >= b0) & (lane < b0 + BIAS_PIECES), 1.0, 0.0).astype(BF16)
        qas.append(jnp.where((lane // dh) == hh, q, one_lanes))
        cts.append(LOG2E * crow_ref[pl.ds(p * hp + hh, 1), :])
        keep.append(jnp.where((feat // dh) == hh, 1.0, 0.0).astype(BF16))
        ones.append(jnp.where(feat == b0, 1.0, 0.0).astype(BF16))
        m_ref[hh] = jnp.full(m_ref.shape[1:], NEG_INF, F32)
        acc_ref[hh] = jnp.zeros(acc_ref.shape[1:], F32)

    def step(kb, masked):
        r0 = pl.multiple_of(kb * t, t)
        vt = vt_ref[:, pl.ds(r0, t)]
        us = [_dot_nt(kaug_ref[hh, pl.ds(r0, t), :], qas[hh]) for hh in range(hp)]
        for hh in range(hp):
            u = us[hh]
            if masked:
                u = jnp.where(key_i <= qry_i, u, NEG_INF)
            m_old = m_ref[hh]
            m_new = jnp.maximum(m_old, jnp.max(u, axis=0, keepdims=True) + cts[hh])
            pe = jnp.exp2(u - (m_new - cts[hh]))
            vh = vt * keep[hh] + ones[hh]
            acc_ref[hh] = jnp.exp2(m_old - m_new) * acc_ref[hh] + _dot(vh, pe.astype(BF16))
            m_ref[hh] = m_new

    def body(kb, carry):
        step(kb, False)
        return carry

    lax.fori_loop(0, qi, body, 0)
    step(qi, True)
    o_t = None
    for hh in range(hp):
        b0 = _bias_lane(hh, dh)
        acc = acc_ref[hh]
        o_h = acc / acc[b0:b0 + 1, :]
        o_t = o_h if o_t is None else jnp.where((feat // dh) == hh, o_h, o_t)
    o_ref[...] = o_t.T.astype(o_ref.dtype)


def _attn_prompt(q, kaug, vt, crow, *, b, l, dh):
    n, fw = q.shape
    t = min(ATTN_TILE, l)
    nq = l // t
    hp = V7X_LANES // dh
    assert l % t == 0 and fw % V7X_LANES == 0 and V7X_LANES % dh == 0 and BIAS_PIECES < dh
    return pl.pallas_call(
        functools.partial(_attn_body, t=t, dh=dh),
        grid=(b, fw // V7X_LANES, nq),
        in_specs=[pl.BlockSpec((t, V7X_LANES), lambda i, p, j: (i * nq + j, p)),
                  pl.BlockSpec((None, hp, l, V7X_LANES), lambda i, p, j: (i, p, 0, 0)),
                  pl.BlockSpec((None, V7X_LANES, l), lambda i, p, j: (i, p, 0)),
                  pl.BlockSpec((None, crow.shape[1], t), lambda i, p, j: (i, 0, j))],
        out_specs=pl.BlockSpec((t, V7X_LANES), lambda i, p, j: (i * nq + j, p)),
        out_shape=jax.ShapeDtypeStruct((n, fw), BF16),
        scratch_shapes=[pltpu.VMEM((hp, 1, t), F32), pltpu.VMEM((hp, V7X_LANES, t), F32)],
        compiler_params=_params("parallel", "parallel", "arbitrary"),
        name="fox_prompt",
    )(q, kaug, vt, crow)


def _gla_masks(c):
    r = np.arange(c)[:, None]
    s = np.arange(c)[None, :]
    tril = (s <= r).astype(np.float32)
    levels = []
    m = c // 2
    while m >= GLA_LEAF:
        levels.append((((r ^ s) < 2 * m) & ((r & m) != 0) & ((s & m) == 0)).astype(np.float32))
        m //= 2
    if not levels:
        levels.append(np.zeros((c, c), np.float32))
    return jnp.asarray(tril, BF16), jnp.asarray(np.stack(levels), F32)


def _gla_chunk(q, k, lf, v, st, tril, lvl_ref, *, c, kd):
    b = _dot01(tril, lf)
    q = q * (kd ** -0.5)
    b_last = b[c - 1:c, :]
    o = _dot_nt((q * jnp.exp(b)).astype(BF16), st.astype(BF16))
    k_hat = (k * jnp.exp(b_last - b)).astype(BF16)
    st_new = st * jnp.exp(b_last) + _dot_tn(v, k_hat)

    m = c // 2
    lvl = 0
    scores = None
    while m >= GLA_LEAF:
        pieces = []
        for blk in range(c // (2 * m)):
            mid = blk * 2 * m + m
            pieces.append(jnp.broadcast_to(b[mid - 1:mid, :], (2 * m, kd)))
        ref = pieces[0] if len(pieces) == 1 else jnp.concatenate(pieces, axis=0)
        e = jnp.exp(-jnp.abs(b - ref))
        part = lvl_ref[lvl] * _dot_nt((q * e).astype(BF16), (k * e).astype(BF16))
        scores = part if scores is None else scores + part
        m //= 2
        lvl += 1
    if scores is not None:
        o = o + _dot(scores.astype(BF16), v)

    nl = c // GLA_LEAF
    b3 = b.reshape(nl, GLA_LEAF, kd)
    q3 = q.reshape(nl, GLA_LEAF, kd)
    k3 = k.reshape(nl, GLA_LEAF, kd)
    v3 = v.astype(F32).reshape(nl, GLA_LEAF, v.shape[-1])
    t_idx = lax.broadcasted_iota(jnp.int32, (nl, GLA_LEAF, kd), 1)
    o3 = jnp.zeros(v3.shape, F32)
    for s in range(GLA_LEAF):
        decay = jnp.exp(jnp.where(t_idx >= s, b3 - b3[:, s:s + 1, :], NEG_INF))
        w = jnp.sum(decay * q3 * k3[:, s:s + 1, :], axis=-1, keepdims=True)
        o3 = o3 + w * v3[:, s:s + 1, :]
    o = o + o3.reshape(c, v.shape[-1])
    return o, st_new


def _gla_body(q_ref, k_ref, lf_ref, v_ref, s0_ref, tril_ref, lvl_ref, o_ref, sT_ref, st_ref,
              *, c, rows, nh, kd, vd):
    j = pl.program_id(1)

    @pl.when(j == 0)
    def _():
        st_ref[...] = s0_ref[...]

    tril = tril_ref[...]

    def chunk(ci, carry):
        r0 = pl.multiple_of(ci * c, c)
        rs = pl.ds(r0, c)
        for h in range(nh):
            ks = slice(h * kd, (h + 1) * kd)
            vs = slice(h * vd, (h + 1) * vd)
            o, st_new = _gla_chunk(q_ref[rs, ks], k_ref[rs, ks], lf_ref[rs, ks], v_ref[rs, vs],
                                   st_ref[h], tril, lvl_ref, c=c, kd=kd)
            o_ref[rs, vs] = o
            st_ref[h] = st_new
        return carry

    if rows == c:
        chunk(0, 0)
    else:
        lax.fori_loop(0, rows // c, chunk, 0)

    @pl.when(j == pl.num_programs(1) - 1)
    def _():
        sT_ref[...] = st_ref[...]


def _gla(q, k, lf, v, s0_t, *, b, l):
    n, hw = q.shape
    hvw = v.shape[1]
    nh, vd, kd = s0_t.shape[1:]
    c = min(GLA_CHUNK, l)
    rows = min(GLA_ROWS, l)
    assert l % rows == 0 and rows % c == 0 and c % GLA_LEAF == 0
    tril, lvls = _gla_masks(c)
    nr = l // rows
    row = lambda w: pl.BlockSpec((rows, w), lambda i, j: (i * nr + j, 0))
    state = pl.BlockSpec((None, nh, vd, kd), lambda i, j: (i, 0, 0, 0))
    return pl.pallas_call(
        functools.partial(_gla_body, c=c, rows=rows, nh=nh, kd=kd, vd=vd),
        grid=(b, nr),
        in_specs=[row(hw), row(hw), row(hw), row(hvw), state, _resident(tril.shape), _resident(lvls.shape)],
        out_specs=[row(hvw), state],
        out_shape=[jax.ShapeDtypeStruct((n, hvw), F32), jax.ShapeDtypeStruct(s0_t.shape, F32)],
        scratch_shapes=[pltpu.VMEM((nh, vd, kd), F32)],
        compiler_params=_params("parallel", "arbitrary"),
        name="hgrn2",
    )(q, k, lf, v, s0_t, tril, lvls)


def _merge_body(x_ref, g_ref, wg_ref, of_ref, oh_ref, hg_ref, og_ref, gsum_ref, wpf_ref, wph_ref, wo_ref,
                o_ref, *, d, vd):
    x = x_ref[...]
    h = _rms_rows(x, g_ref[...]).astype(BF16)
    gate_fox = _sigmoid(_dot(h, wg_ref[:, :d]))
    gate_hg = _sigmoid(_dot(h, wg_ref[:, d:]))
    y_fox = _dot(of_ref[...], wpf_ref[...])
    oh = oh_ref[...]
    ms = _dot((oh * oh).astype(BF16), gsum_ref[...]) * (1.0 / vd)
    hg = hg_ref[...]
    ob = (oh * lax.rsqrt(ms + EPS) * og_ref[...]) * (hg * _sigmoid(hg))
    y_hg = _dot(ob.astype(BF16), wph_ref[...])
    y = gate_fox * y_fox + gate_hg * y_hg
    o_ref[...] = x + _dot(y.astype(BF16), wo_ref[...])


def _merge(x, gain, w_gate, o_fox, o_hg, hgate, o_gain, gsum, w_pf, w_ph, w_out, *, vd):
    n, d = x.shape
    tm = min(ROW_TILE, n)
    assert n % tm == 0
    row = lambda w: pl.BlockSpec((tm, w), lambda i: (i, 0))
    return pl.pallas_call(
        functools.partial(_merge_body, d=d, vd=vd),
        grid=(n // tm,),
        in_specs=[row(d), _resident((1, d)), _resident(w_gate.shape), row(o_fox.shape[1]), row(o_hg.shape[1]),
                  row(hgate.shape[1]), _resident(o_gain.shape), _resident(gsum.shape),
                  _resident(w_pf.shape), _resident(w_ph.shape), _resident(w_out.shape)],
        out_specs=row(d),
        out_shape=jax.ShapeDtypeStruct((n, d), F32),
        compiler_params=_params("parallel"),
        name="merge",
    )(x, gain, w_gate, o_fox, o_hg, hgate, o_gain, gsum, w_pf, w_ph, w_out)


def _paged_body(pt_ref, *refs, g, ps, nh, dh, t):
    k_refs = refs[:g]
    v_refs = refs[g:2 * g]
    lf_refs = refs[2 * g:3 * g]
    (w_ref, q_ref, kn_ref, vn_ref, lfn_ref, o_ref,
     qbd_ref, ctcol_ref, carry_ref, m_ref, l_ref, acc_ref) = refs[3 * g:]
    j = pl.program_id(1)
    rows = nh * t
    fw = nh * dh
    row_head = lax.broadcasted_iota(jnp.int32, (rows, 1), 0) // t

    def new_token_cumsum():
        lfn = lfn_ref[...]
        sub = lax.broadcasted_iota(jnp.int32, lfn.shape, 0)
        ct = jnp.zeros_like(lfn)
        for s in range(t):
            ct = ct + jnp.where(sub >= s, lfn[s:s + 1, :], 0.0)
        return ct

    @pl.when(j == 0)
    def _():
        lane_head = lax.broadcasted_iota(jnp.int32, (rows, fw), 1) // dh
        q_rep = jnp.concatenate([q_ref[...]] * nh, axis=0)
        qbd_ref[...] = jnp.where(lane_head == row_head, q_rep, jnp.zeros_like(q_rep))
        ct_rep = jnp.concatenate([new_token_cumsum()] * nh, axis=0)
        lane = lax.broadcasted_iota(jnp.int32, ct_rep.shape, 1)
        ctcol_ref[...] = LOG2E * jnp.sum(jnp.where(lane == row_head, ct_rep, 0.0), axis=-1, keepdims=True)
        carry_ref[...] = jnp.zeros_like(carry_ref)
        m_ref[...] = jnp.full_like(m_ref, NEG_INF)
        l_ref[...] = jnp.zeros_like(l_ref)
        acc_ref[...] = jnp.zeros_like(acc_ref)

    qbd = qbd_ref[...]
    ctcol = ctcol_ref[...]

    def update(u, pv_fn):
        m_old = m_ref[...]
        m_new = jnp.maximum(m_old, jnp.max(u, axis=-1, keepdims=True) + ctcol)
        pe = jnp.exp2(u - (m_new - ctcol))
        alpha = jnp.exp2(m_old - m_new)
        l_ref[...] = alpha * l_ref[...] + jnp.sum(pe, axis=-1, keepdims=True)
        acc_ref[...] = alpha * acc_ref[...] + pv_fn(pe)
        m_ref[...] = m_new

    lf_all = jnp.concatenate([lf_refs[i][...] for i in range(g)], axis=0)
    sums = _dot01_r(lf_all, w_ref[...])
    carry = carry_ref[...]
    biases = [None] * g
    for i in reversed(range(g)):
        biases[i] = sums[i * nh:(i + 1) * nh, :ps] + carry
        carry = carry + sums[i * nh:(i + 1) * nh, ps:]
    carry_ref[...] = carry
    bias = LOG2E * jnp.concatenate(biases, axis=1)
    bias = jnp.broadcast_to(bias[:, None, :], (nh, t, g * ps)).reshape(rows, g * ps)

    s = jnp.concatenate([_dot(qbd, k_refs[i][...].astype(BF16)) for i in range(g)], axis=1)

    def pv_past(pe):
        out = None
        for i in range(g):
            part = _dot_nt(pe[:, i * ps:(i + 1) * ps].astype(BF16), v_refs[i][...].astype(BF16))
            out = part if out is None else out + part
        return out

    update(s + bias, pv_past)

    @pl.when(j == pl.num_programs(1) - 1)
    def _():
        pad = jnp.zeros((ps - t, fw), BF16)
        kn = jnp.concatenate([kn_ref[...], pad], axis=0)
        vn = jnp.concatenate([vn_ref[...], pad], axis=0)
        ct = new_token_cumsum()
        ct_pad = jnp.concatenate([ct, jnp.zeros((ps - t, ct.shape[1]), F32)], axis=0)
        lane = lax.broadcasted_iota(jnp.int32, (rows, ct.shape[1]), 1)
        onehot = jnp.where(lane == row_head, 1.0, 0.0).astype(BF16)
        c_s = _dot01_nt(onehot, ct_pad)
        col = lax.broadcasted_iota(jnp.int32, (rows, ps), 1)
        row_t = lax.broadcasted_iota(jnp.int32, (rows, ps), 0) % t
        u = jnp.where(col <= row_t, _dot_nt(qbd, kn) - LOG2E * c_s, NEG_INF)
        update(u, lambda pe: _dot(pe.astype(BF16), vn))
        o = acc_ref[...] / l_ref[...]
        lane_head = lax.broadcasted_iota(jnp.int32, (rows, fw), 1) // dh
        o = jnp.where(lane_head == row_head, o, 0.0)
        o_ref[...] = jnp.sum(o.reshape(nh, t, fw), axis=0).astype(o_ref.dtype)


def _attn_paged(cache_kt, cache_vt, cache_lft, page_table, q, k_new, v_new, lf_new, *, layer, t, dh):
    _, _, fw, ps = cache_kt.shape
    nb, n_pages = page_table.shape
    nh = fw // dh
    g = min(PAGES_PER_STEP, n_pages)
    ng = n_pages // g
    rows = nh * t
    assert n_pages % g == 0 and t <= ps and cache_lft.shape[2:] == (nh, ps)
    j_idx = np.arange(ps)
    w = np.concatenate([(j_idx[:, None] > j_idx[None, :]), np.ones((ps, ps), bool)], axis=1)
    w = jnp.asarray(w.astype(np.float32), BF16)

    def page(i, width):
        return pl.BlockSpec((None, None, width, ps),
                            lambda b, j, pt: (layer, pt[b, (ng - 1 - j) * g + i], 0, 0))

    new_rows = lambda width: pl.BlockSpec((t, width), lambda b, j, pt: (b, 0))
    grid_spec = pltpu.PrefetchScalarGridSpec(
        num_scalar_prefetch=1,
        grid=(nb, ng),
        in_specs=[page(i, fw) for i in range(g)] + [page(i, fw) for i in range(g)]
        + [page(i, nh) for i in range(g)]
        + [pl.BlockSpec(w.shape, lambda b, j, pt: (0, 0)),
           new_rows(fw), new_rows(fw), new_rows(fw), new_rows(lf_new.shape[1])],
        out_specs=new_rows(fw),
        scratch_shapes=[pltpu.VMEM((rows, fw), BF16), pltpu.VMEM((rows, 1), F32), pltpu.VMEM((nh, ps), F32),
                        pltpu.VMEM((rows, 1), F32), pltpu.VMEM((rows, 1), F32), pltpu.VMEM((rows, fw), F32)],
    )
    return pl.pallas_call(
        functools.partial(_paged_body, g=g, ps=ps, nh=nh, dh=dh, t=t),
        grid_spec=grid_spec,
        out_shape=jax.ShapeDtypeStruct((nb * t, fw), BF16),
        compiler_params=_params("parallel", "arbitrary"),
        name="fox_paged",
    )(page_table, *([cache_kt] * g), *([cache_vt] * g), *([cache_lft] * g), w, q, k_new, v_new, lf_new)


def _block_ones(width, group):
    i = np.arange(width) // group
    return jnp.asarray((i[:, None] == i[None, :]).astype(np.float32), BF16)


def kernel(x_prompt, x_sample, cache_k, cache_v, cache_logf, state_hgrn, page_table, ffn1_norm, ffn1_w_in, ffn1_w_out, mix_norm, w_in, fox_f_bias, fox_q_gain, fox_k_gain, hg_lb_logits, hg_o_gain, w_proj_fox, w_proj_hg, w_out, ffn2_norm, ffn2_w_in, ffn2_w_out):
    depth = w_in.shape[0]
    bp, lp, d = x_prompt.shape
    bs, ls, _ = x_sample.shape
    nh, dh = cache_k.shape[-2:]
    fw = nh * dh
    hg_heads, kd, vd = state_hgrn.shape[-3:]
    hw, hvw = hg_heads * kd, hg_heads * vd
    assert nh <= V7X_LANES and w_in.shape[-1] == 3 * fw + nh + 2 * hw + 2 * hvw + 2 * d

    gsum_fox = _block_ones(fw, dh)
    gsum_hg = _block_ones(hvw, vd)
    tc = min(ROW_TILE, lp)
    tril_seq = jnp.asarray(np.tril(np.ones((tc, tc), np.float32)), BF16)
    lb_logits = hg_lb_logits.astype(F32)
    cache_kt = jnp.transpose(cache_k, (0, 1, 3, 4, 2)).reshape(cache_k.shape[:2] + (fw, cache_k.shape[2]))
    cache_vt = jnp.transpose(cache_v, (0, 1, 3, 4, 2)).reshape(cache_v.shape[:2] + (fw, cache_v.shape[2]))
    cache_lft = jnp.transpose(cache_logf, (0, 1, 3, 2)).astype(F32)

    yp = x_prompt.reshape(bp * lp, d)
    ys = x_sample.reshape(bs * ls, d)
    outs = {name: [] for name in ("kp", "vp", "fp", "sp", "ks", "vs", "fs", "ss")}

    for l in range(depth):
        c_mix = 3 * fw + nh
        w_l = w_in[l]
        w_mix = jnp.concatenate(
            [w_l[:, :3 * fw], w_l[:, c_mix:c_mix + 2 * hw + 2 * hvw],
             jnp.pad(w_l[:, 3 * fw:c_mix], ((0, 0), (0, V7X_LANES - nh)))], axis=1).astype(BF16)
        w_gate = w_l[:, c_mix + 2 * hw + 2 * hvw:].astype(BF16)
        f_bias = jnp.pad(fox_f_bias[l], (0, V7X_LANES - nh)).reshape(1, V7X_LANES)
        q_gain = jnp.tile(fox_q_gain[l], nh).reshape(1, fw)
        k_gain = jnp.tile(fox_k_gain[l], nh).reshape(1, fw)
        o_gain = jnp.tile(hg_o_gain[l], hg_heads).reshape(1, hvw)
        mix_gain = mix_norm[l].reshape(1, d)
        w1_in, w1_out = ffn1_w_in[l].astype(BF16), ffn1_w_out[l].astype(BF16)
        w2_in, w2_out = ffn2_w_in[l].astype(BF16), ffn2_w_out[l].astype(BF16)
        w_pf, w_ph, w_o = w_proj_fox[l].astype(BF16), w_proj_hg[l].astype(BF16), w_out[l].astype(BF16)

        yp = _ffn(yp, ffn1_norm[l].reshape(1, d), w1_in, w1_out)
        ys = _ffn(ys, ffn1_norm[l].reshape(1, d), w1_in, w1_out)

        inproj = functools.partial(_inproj, gain=mix_gain, w_mix=w_mix, gsum=gsum_fox, q_gain=q_gain,
                                   k_gain=k_gain, f_bias=f_bias, lb_logits=lb_logits,
                                   fw=fw, hw=hw, hvw=hvw, dh=dh, layer=l)
        merge = functools.partial(_merge, gain=mix_gain, w_gate=w_gate, o_gain=o_gain, gsum=gsum_hg,
                                  w_pf=w_pf, w_ph=w_ph, w_out=w_o, vd=vd)

        q, kf, vf, kb, vb, lf, hq, hk, hlf, hv, hgate = inproj(yp)
        c, kaug = _cumsum(lf.reshape(bp, lp, V7X_LANES), kb, tril_seq, nh=nh, dh=dh)
        c_row = jnp.transpose(c[:, :, :nh], (0, 2, 1))
        v_t = jnp.transpose(vb.reshape(bp, lp, fw), (0, 2, 1))
        o_fox = _attn_prompt(q, kaug, v_t, c_row, b=bp, l=lp, dh=dh)
        s0_t = jnp.zeros((bp, hg_heads, vd, kd), F32)
        o_hg, s_t = _gla(hq, hk, hlf, hv, s0_t, b=bp, l=lp)
        yp = merge(yp, o_fox=o_fox, o_hg=o_hg, hgate=hgate)
        outs["kp"].append(kf.reshape(bp, lp, nh, dh))
        outs["vp"].append(vf.reshape(bp, lp, nh, dh))
        outs["fp"].append(lf[:, :nh].reshape(bp, lp, nh).astype(cache_logf.dtype))
        outs["sp"].append(jnp.swapaxes(s_t, -1, -2).astype(x_prompt.dtype))

        q, kf, vf, kb, vb, lf, hq, hk, hlf, hv, hgate = inproj(ys)
        o_fox = _attn_paged(cache_kt, cache_vt, cache_lft, page_table, q, kb, vb, lf, layer=l, t=ls, dh=dh)
        s0_t = jnp.swapaxes(state_hgrn[l].astype(F32), -1, -2)
        o_hg, s_t = _gla(hq, hk, hlf, hv, s0_t, b=bs, l=ls)
        ys = merge(ys, o_fox=o_fox, o_hg=o_hg, hgate=hgate)
        outs["ks"].append(kf.reshape(bs, ls, nh, dh))
        outs["vs"].append(vf.reshape(bs, ls, nh, dh))
        outs["fs"].append(lf[:, :nh].reshape(bs, ls, nh).astype(cache_logf.dtype))
        outs["ss"].append(jnp.swapaxes(s_t, -1, -2).astype(state_hgrn.dtype))

        yp = _ffn(yp, ffn2_norm[l].reshape(1, d), w2_in, w2_out)
        ys = _ffn(ys, ffn2_norm[l].reshape(1, d), w2_in, w2_out)

    st = lambda name: jnp.stack(outs[name])
    return (yp.reshape(bp, lp, d), ys.reshape(bs, ls, d), st("kp"), st("vp"), st("fp"), st("sp"),
            st("ks"), st("vs"), st("fs"), st("ss"))
```

```python
import functools

import jax
import jax.numpy as jnp
import numpy as np
from jax import lax
from jax.experimental import pallas as pl
from jax.experimental.pallas import tpu as pltpu

F32 = jnp.float32
BF16 = jnp.bfloat16
EPS = 1e-6
NEG_INF = float("-inf")
LOG2E = 1.4426950408889634

V7X_LANES = 128
V7X_SUBLANES = 8
V7X_VMEM_BYTES = 64 * 1024 * 1024
VMEM_LIMIT = (V7X_VMEM_BYTES * 3) // 4

ROW_TILE = 512
ATTN_TILE = 512
ATTN_UNROLL = 2
GLA_CHUNK = 64
GLA_LEAF = V7X_SUBLANES
GLA_ROWS = 512
PAGES_PER_STEP = 16


def _params(*sem):
    return pltpu.CompilerParams(dimension_semantics=sem, vmem_limit_bytes=VMEM_LIMIT)


def _resident(shape):
    nd = len(shape)
    return pl.BlockSpec(shape, lambda *_: (0,) * nd, pipeline_mode=pl.Buffered(1))


def _dot(a, b):
    return jnp.dot(a, b, preferred_element_type=F32)


def _dot_nt(a, b):
    return lax.dot_general(a, b, (((1,), (1,)), ((), ())), preferred_element_type=F32)


def _dot_tn(a, b):
    return lax.dot_general(a, b, (((0,), (0,)), ((), ())), preferred_element_type=F32)


def _split3(x):
    hi = x.astype(BF16)
    r1 = x - hi.astype(F32)
    mid = r1.astype(BF16)
    lo = (r1 - mid.astype(F32)).astype(BF16)
    return hi, mid, lo


def _dot01(m01, x):
    hi, mid, lo = _split3(x)
    return _dot(m01, hi) + _dot(m01, mid) + _dot(m01, lo)


def _dot01_nt(m01, x):
    hi, mid, lo = _split3(x)
    return _dot_nt(m01, hi) + _dot_nt(m01, mid) + _dot_nt(m01, lo)


def _dot01_r(x, m01):
    hi, mid, lo = _split3(x)
    return _dot(hi, m01) + _dot(mid, m01) + _dot(lo, m01)


def _rms_rows(x, gain):
    return x * lax.rsqrt(jnp.mean(x * x, axis=-1, keepdims=True) + EPS) * gain


def _sigmoid(x):
    return 1.0 / (1.0 + jnp.exp(-x))


def _ffn_body(x_ref, g_ref, wi_ref, wo_ref, o_ref, acc_ref, *, d_ff, tf):
    x = x_ref[...]
    h = _rms_rows(x, g_ref[...]).astype(BF16)
    for j in range(d_ff // tf):
        a = _dot(h, wi_ref[:, j * tf:(j + 1) * tf])
        b = _dot(h, wi_ref[:, d_ff + j * tf:d_ff + (j + 1) * tf])
        g = (a * _sigmoid(a) * b).astype(BF16)
        y = _dot(g, wo_ref[j * tf:(j + 1) * tf, :])
        if j == 0:
            acc_ref[...] = y
        else:
            acc_ref[...] += y
    o_ref[...] = x + 0.5 * acc_ref[...]


def _ffn(x, gain, w_in, w_out):
    n, d = x.shape
    d_ff = w_out.shape[0]
    tm = min(ROW_TILE, n)
    tf = 2 * V7X_LANES if d_ff % (2 * V7X_LANES) == 0 else V7X_LANES
    assert n % tm == 0 and d_ff % tf == 0
    return pl.pallas_call(
        functools.partial(_ffn_body, d_ff=d_ff, tf=tf),
        grid=(n // tm,),
        in_specs=[pl.BlockSpec((tm, d), lambda i: (i, 0)),
                  _resident((1, d)),
                  _resident((d, 2 * d_ff)),
                  _resident((d_ff, d))],
        out_specs=pl.BlockSpec((tm, d), lambda i: (i, 0)),
        out_shape=jax.ShapeDtypeStruct((n, d), F32),
        scratch_shapes=[pltpu.VMEM((tm, d), F32)],
        compiler_params=_params("parallel"),
        name="ffn",
    )(x, gain, w_in, w_out)


def _inproj_body(x_ref, g_ref, w_ref, gsum_ref, qg_ref, kg_ref, fb_ref, lbl_ref,
                 q_ref, kf_ref, vf_ref, kb_ref, vb_ref, lf_ref,
                 hq_ref, hk_ref, hlf_ref, hv_ref, hg_ref, *, fw, hw, hvw, dh, layer):
    h = _rms_rows(x_ref[...], g_ref[...]).astype(BF16)

    def proj(c0, n):
        return _dot(h, w_ref[:, c0:c0 + n])

    gsum = gsum_ref[...]

    def head_norm(z, gain):
        ms = _dot((z * z).astype(BF16), gsum) * (1.0 / dh)
        return z * lax.rsqrt(ms + EPS) * gain

    q = head_norm(proj(0, fw), qg_ref[...])
    q_ref[...] = (q * (dh ** -0.5 * LOG2E)).astype(BF16)
    k = head_norm(proj(fw, fw), kg_ref[...])
    kf_ref[...] = k
    kb_ref[...] = k.astype(BF16)
    v = proj(2 * fw, fw)
    vf_ref[...] = v
    vb_ref[...] = v.astype(BF16)

    c0 = 3 * fw
    hq = proj(c0, hw)
    hq_ref[...] = hq * _sigmoid(hq)

    lg = lbl_ref[...]
    e = jnp.exp(lg - jnp.max(lg, axis=0, keepdims=True))
    sm = e / jnp.sum(e, axis=0, keepdims=True)
    lb = jnp.zeros_like(sm[0:1])
    for i in range(1, layer + 1):
        lb = lb + sm[i:i + 1]
    log_lb = jnp.log(lb)
    log_1m_lb = jnp.log1p(-lb)

    z = proj(c0 + hw, hw)
    ez = jnp.exp(-jnp.abs(z))
    log_sig = jnp.minimum(z, 0.0) - jnp.log1p(ez)
    t = log_1m_lb + log_sig
    mx = jnp.maximum(log_lb, t)
    hlf_ref[...] = mx + jnp.log1p(jnp.exp(-jnp.abs(log_lb - t)))
    hk_ref[...] = (1.0 - lb) * (jnp.where(z >= 0.0, ez, 1.0) / (1.0 + ez))

    hv_ref[...] = proj(c0 + 2 * hw, hvw).astype(BF16)
    hg_ref[...] = proj(c0 + 2 * hw + hvw, hvw)

    zf = proj(c0 + 2 * hw + 2 * hvw, V7X_LANES) + fb_ref[...]
    lf_ref[...] = jnp.minimum(zf, 0.0) - jnp.log1p(jnp.exp(-jnp.abs(zf)))


def _inproj(x, gain, w_mix, gsum, q_gain, k_gain, f_bias, lb_logits, *, fw, hw, hvw, dh, layer):
    n, d = x.shape
    tm = min(ROW_TILE, n)
    assert n % tm == 0
    row = lambda w: pl.BlockSpec((tm, w), lambda i: (i, 0))
    sds = lambda w, dt: jax.ShapeDtypeStruct((n, w), dt)
    return pl.pallas_call(
        functools.partial(_inproj_body, fw=fw, hw=hw, hvw=hvw, dh=dh, layer=layer),
        grid=(n // tm,),
        in_specs=[row(d), _resident((1, d)), _resident(w_mix.shape), _resident(gsum.shape),
                  _resident((1, fw)), _resident((1, fw)), _resident((1, V7X_LANES)),
                  _resident(lb_logits.shape)],
        out_specs=[row(fw), row(fw), row(fw), row(fw), row(fw), row(V7X_LANES),
                   row(hw), row(hw), row(hw), row(hvw), row(hvw)],
        out_shape=[sds(fw, BF16), sds(fw, F32), sds(fw, F32), sds(fw, BF16), sds(fw, BF16),
                   sds(V7X_LANES, F32),
                   sds(hw, F32), sds(hw, F32), sds(hw, F32), sds(hvw, BF16), sds(hvw, F32)],
        compiler_params=_params("parallel"),
        name="inproj",
    )(x, gain, w_mix, gsum, q_gain, k_gain, f_bias, lb_logits)


BIAS_PIECES = 3


def _bias_lane(hh, dh):
    return ((hh + 1) % (V7X_LANES // dh)) * dh


def _cumsum_body(lf_ref, k_ref, tril_ref, place_ref, c_ref, kaug_ref, carry_ref, *, nh, dh):
    @pl.when(pl.program_id(1) == 0)
    def _():
        carry_ref[...] = jnp.zeros_like(carry_ref)

    c = _dot01(tril_ref[...], lf_ref[...]) + carry_ref[...]
    c_ref[...] = c
    carry_ref[...] = c[-1:, :]
    pieces = jnp.concatenate(_split3(-LOG2E * c), axis=1)
    lane = lax.broadcasted_iota(jnp.int32, (1, V7X_LANES), 1)
    hp = V7X_LANES // dh
    for h in range(nh):
        tile = h // hp
        placed = _dot(pieces, place_ref[h]).astype(BF16)
        keys = k_ref[:, tile * V7X_LANES:(tile + 1) * V7X_LANES]
        kaug_ref[h] = jnp.where((lane // dh) == (h % hp), keys, placed)


def _cumsum(lf, k, tril, *, nh, dh):
    b, l, w = lf.shape
    tc = tril.shape[0]
    nt = l // tc
    place = np.zeros((nh, BIAS_PIECES * w, V7X_LANES), np.float32)
    for h in range(nh):
        for piece in range(BIAS_PIECES):
            place[h, piece * w + h, _bias_lane(h % (V7X_LANES // dh), dh) + piece] = 1.0
    place = jnp.asarray(place, BF16)
    return pl.pallas_call(
        functools.partial(_cumsum_body, nh=nh, dh=dh),
        grid=(b, nt),
        in_specs=[pl.BlockSpec((None, tc, w), lambda i, j: (i, j, 0)),
                  pl.BlockSpec((tc, k.shape[1]), lambda i, j: (i * nt + j, 0)),
                  _resident(tril.shape), _resident(place.shape)],
        out_specs=[pl.BlockSpec((None, tc, w), lambda i, j: (i, j, 0)),
                   pl.BlockSpec((None, nh, tc, V7X_LANES), lambda i, j: (i, 0, j, 0))],
        out_shape=[jax.ShapeDtypeStruct((b, l, w), F32), jax.ShapeDtypeStruct((b, nh, l, V7X_LANES), BF16)],
        scratch_shapes=[pltpu.VMEM((1, w), F32)],
        compiler_params=_params("parallel", "arbitrary"),
        name="logf_cumsum",
    )(lf, k, tril, place)


def _attn_body(q_ref, kaug_ref, vt_ref, crow_ref, o_ref, m_ref, acc_ref, *, t, dh):
    p = pl.program_id(1)
    qi = pl.program_id(2)
    hp = V7X_LANES // dh
    q = q_ref[...]
    lane = lax.broadcasted_iota(jnp.int32, (1, V7X_LANES), 1)
    key_i = lax.broadcasted_iota(jnp.int32, (t, t), 0)
    qry_i = lax.broadcasted_iota(jnp.int32, (t, t), 1)
    feat = lax.broadcasted_iota(jnp.int32, (V7X_LANES, t), 0)
    qas, cts, keep, ones = [], [], [], []
    for hh in range(hp):
        b0 = _bias_lane(hh, dh)
        one_lanes = jnp.where((lane >= b0) & (lane < b0 + BIAS_PIECES), 1.0, 0.0).astype(BF16)
        qas.append(jnp.where((lane // dh) == hh, q, one_lanes))
        cts.append(LOG2E * crow_ref[pl.ds(p * hp + hh, 1), :])
        keep.append(jnp.where((feat // dh) == hh, 1.0, 0.0).astype(BF16))
        ones.append(jnp.where(feat == b0, 1.0, 0.0).astype(BF16))
        m_ref[hh] = jnp.full(m_ref.shape[1:], NEG_INF, F32)
        acc_ref[hh] = jnp.zeros(acc_ref.shape[1:], F32)

    def steps(kbs, diagonal_last):
        r0s = [pl.multiple_of(kb * t, t) for kb in kbs]
        us = [[_dot_nt(kaug_ref[hh, pl.ds(r0, t), :], qas[hh]) for hh in range(hp)] for r0 in r0s]
        heads = range(hp)
        for n, (r0, u_heads) in enumerate(zip(r0s, us)):
            vt = vt_ref[:, pl.ds(r0, t)]
            if diagonal_last and n == len(kbs) - 1:
                u_heads = [jnp.where(key_i <= qry_i, u, NEG_INF) for u in u_heads]
            m_old = [m_ref[hh] for hh in heads]
            m_new = [jnp.maximum(m_old[hh], jnp.max(u_heads[hh], axis=0, keepdims=True) + cts[hh]) for hh in heads]
            pe = [jnp.exp2(u_heads[hh] - (m_new[hh] - cts[hh])).astype(BF16) for hh in heads]
            pv = [_dot(vt * keep[hh] + ones[hh], pe[hh]) for hh in heads]
            for hh in heads:
                acc_ref[hh] = jnp.exp2(m_old[hh] - m_new[hh]) * acc_ref[hh] + pv[hh]
                m_ref[hh] = m_new[hh]

    def body(i, carry):
        steps([ATTN_UNROLL * i + r for r in range(ATTN_UNROLL)], False)
        return carry

    lax.fori_loop(0, qi // ATTN_UNROLL, body, 0)
    for rem in range(ATTN_UNROLL):
        @pl.when(qi % ATTN_UNROLL == rem)
        def _(rem=rem):
            steps([qi - rem + r for r in range(rem)] + [qi], True)
    o_t = None
    for hh in range(hp):
        b0 = _bias_lane(hh, dh)
        acc = acc_ref[hh]
        o_h = acc / acc[b0:b0 + 1, :]
        o_t = o_h if o_t is None else jnp.where((feat // dh) == hh, o_h, o_t)
    o_ref[...] = o_t.T.astype(o_ref.dtype)


def _attn_prompt(q, kaug, vt, crow, *, b, l, dh):
    n, fw = q.shape
    t = min(ATTN_TILE, l)
    nq = l // t
    hp = V7X_LANES // dh
    assert l % t == 0 and fw % V7X_LANES == 0 and V7X_LANES % dh == 0 and BIAS_PIECES < dh
    return pl.pallas_call(
        functools.partial(_attn_body, t=t, dh=dh),
        grid=(b, fw // V7X_LANES, nq),
        in_specs=[pl.BlockSpec((t, V7X_LANES), lambda i, p, j: (i * nq + j, p)),
                  pl.BlockSpec((None, hp, l, V7X_LANES), lambda i, p, j: (i, p, 0, 0)),
                  pl.BlockSpec((None, V7X_LANES, l), lambda i, p, j: (i, p, 0)),
                  pl.BlockSpec((None, crow.shape[1], t), lambda i, p, j: (i, 0, j))],
        out_specs=pl.BlockSpec((t, V7X_LANES), lambda i, p, j: (i * nq + j, p)),
        out_shape=jax.ShapeDtypeStruct((n, fw), BF16),
        scratch_shapes=[pltpu.VMEM((hp, 1, t), F32), pltpu.VMEM((hp, V7X_LANES, t), F32)],
        compiler_params=_params("parallel", "parallel", "arbitrary"),
        name="fox_prompt",
    )(q, kaug, vt, crow)


def _gla_masks(c):
    r = np.arange(c)[:, None]
    s = np.arange(c)[None, :]
    tril = (s <= r).astype(np.float32)
    levels = []
    m = c // 2
    while m >= GLA_LEAF:
        levels.append((((r ^ s) < 2 * m) & ((r & m) != 0) & ((s & m) == 0)).astype(np.float32))
        m //= 2
    if not levels:
        levels.append(np.zeros((c, c), np.float32))
    return jnp.asarray(tril, BF16), jnp.asarray(np.stack(levels), F32)


def _gla_chunk(q, k, lf, v, states, tril, lvl_ref, leaf_refs, *, c, nh, kd, vd):
    def head(x, h, w):
        return x[:, h * w:(h + 1) * w]

    b = _dot01(tril, LOG2E * lf)
    q = q * (kd ** -0.5)
    b_last = b[c - 1:c, :]
    q_hat = (q * jnp.exp2(b)).astype(BF16)
    k_hat = (k * jnp.exp2(b_last - b)).astype(BF16)
    decay_last = jnp.exp2(b_last)
    outs = [_dot_nt(head(q_hat, h, kd), states[h].astype(BF16)) for h in range(nh)]
    new_states = [states[h] * head(decay_last, h, kd) + _dot_tn(head(v, h, vd), head(k_hat, h, kd))
                  for h in range(nh)]

    m = c // 2
    lvl = 0
    scores = [None] * nh
    while m >= GLA_LEAF:
        pieces = []
        for blk in range(c // (2 * m)):
            mid = blk * 2 * m + m
            pieces.append(jnp.broadcast_to(b[mid - 1:mid, :], (2 * m, b.shape[1])))
        ref = pieces[0] if len(pieces) == 1 else jnp.concatenate(pieces, axis=0)
        e = jnp.exp2(-jnp.abs(b - ref))
        qe = (q * e).astype(BF16)
        ke = (k * e).astype(BF16)
        mask = lvl_ref[lvl]
        for h in range(nh):
            part = mask * _dot_nt(head(qe, h, kd), head(ke, h, kd))
            scores[h] = part if scores[h] is None else scores[h] + part
        m //= 2
        lvl += 1
    if scores[0] is not None:
        outs = [outs[h] + _dot(scores[h].astype(BF16), head(v, h, vd)) for h in range(nh)]

    nl = c // GLA_LEAF
    b_ref, q_ref, k_ref, v_ref, o_ref = leaf_refs
    v32 = v.astype(F32)
    for h in range(nh):
        b_ref[h] = head(b, h, kd)
        q_ref[h] = head(q, h, kd)
        k_ref[h] = head(k, h, kd)
        v_ref[h] = head(v32, h, vd)

    def rows(ref, i):
        return jnp.concatenate([ref[h, pl.ds(i, nl, stride=GLA_LEAF), :] for h in range(nh)], axis=1)

    bs = [rows(b_ref, i) for i in range(GLA_LEAF)]
    ks = [rows(k_ref, i) for i in range(GLA_LEAF)]
    vs = [rows(v_ref, i) for i in range(GLA_LEAF)]
    for t in range(GLA_LEAF):
        q_t = rows(q_ref, t)
        o_t = [None] * nh
        for s in range(t + 1):
            prod = jnp.exp2(bs[t] - bs[s]) * q_t * ks[s]
            for h in range(nh):
                w = jnp.sum(head(prod, h, kd), axis=-1, keepdims=True)
                term = w * head(vs[s], h, vd)
                o_t[h] = term if o_t[h] is None else o_t[h] + term
        for h in range(nh):
            o_ref[h, pl.ds(t, nl, stride=GLA_LEAF), :] = o_t[h]
    return jnp.concatenate([outs[h] + o_ref[h] for h in range(nh)], axis=1), new_states


def _gla_body(q_ref, k_ref, lf_ref, v_ref, s0_ref, tril_ref, lvl_ref, o_ref, sT_ref, st_ref, *leaf_refs,
              c, rows, nh, kd, vd):
    j = pl.program_id(1)

    @pl.when(j == 0)
    def _():
        st_ref[...] = s0_ref[...]

    tril = tril_ref[...]

    def chunk(ci, carry):
        rs = pl.ds(pl.multiple_of(ci * c, c), c)
        o, new_states = _gla_chunk(q_ref[rs, :], k_ref[rs, :], lf_ref[rs, :], v_ref[rs, :],
                                   [st_ref[h] for h in range(nh)], tril, lvl_ref, leaf_refs,
                                   c=c, nh=nh, kd=kd, vd=vd)
        o_ref[rs, :] = o
        for h in range(nh):
            st_ref[h] = new_states[h]
        return carry

    if rows == c:
        chunk(0, 0)
    else:
        lax.fori_loop(0, rows // c, chunk, 0)

    @pl.when(j == pl.num_programs(1) - 1)
    def _():
        sT_ref[...] = st_ref[...]


def _gla(q, k, lf, v, s0_t, *, b, l):
    n, hw = q.shape
    hvw = v.shape[1]
    nh, vd, kd = s0_t.shape[1:]
    c = min(GLA_CHUNK, l)
    rows = min(GLA_ROWS, l)
    assert l % rows == 0 and rows % c == 0 and c % GLA_LEAF == 0
    tril, lvls = _gla_masks(c)
    nr = l // rows
    row = lambda w: pl.BlockSpec((rows, w), lambda i, j: (i * nr + j, 0))
    state = pl.BlockSpec((None, nh, vd, kd), lambda i, j: (i, 0, 0, 0))
    return pl.pallas_call(
        functools.partial(_gla_body, c=c, rows=rows, nh=nh, kd=kd, vd=vd),
        grid=(b, nr),
        in_specs=[row(hw), row(hw), row(hw), row(hvw), state, _resident(tril.shape), _resident(lvls.shape)],
        out_specs=[row(hvw), state],
        out_shape=[jax.ShapeDtypeStruct((n, hvw), F32), jax.ShapeDtypeStruct(s0_t.shape, F32)],
        scratch_shapes=[pltpu.VMEM((nh, vd, kd), F32)]
        + [pltpu.VMEM((nh, c, w), F32) for w in (kd, kd, kd, vd, vd)],
        compiler_params=_params("parallel", "arbitrary"),
        name="hgrn2",
    )(q, k, lf, v, s0_t, tril, lvls)


def _merge_body(x_ref, g_ref, wg_ref, of_ref, oh_ref, hg_ref, og_ref, gsum_ref, wpf_ref, wph_ref, wo_ref,
                o_ref, *, d, vd):
    x = x_ref[...]
    h = _rms_rows(x, g_ref[...]).astype(BF16)
    gate_fox = _sigmoid(_dot(h, wg_ref[:, :d]))
    gate_hg = _sigmoid(_dot(h, wg_ref[:, d:]))
    y_fox = _dot(of_ref[...], wpf_ref[...])
    oh = oh_ref[...]
    ms = _dot((oh * oh).astype(BF16), gsum_ref[...]) * (1.0 / vd)
    hg = hg_ref[...]
    ob = (oh * lax.rsqrt(ms + EPS) * og_ref[...]) * (hg * _sigmoid(hg))
    y_hg = _dot(ob.astype(BF16), wph_ref[...])
    y = gate_fox * y_fox + gate_hg * y_hg
    o_ref[...] = x + _dot(y.astype(BF16), wo_ref[...])


def _merge(x, gain, w_gate, o_fox, o_hg, hgate, o_gain, gsum, w_pf, w_ph, w_out, *, vd):
    n, d = x.shape
    tm = min(ROW_TILE, n)
    assert n % tm == 0
    row = lambda w: pl.BlockSpec((tm, w), lambda i: (i, 0))
    return pl.pallas_call(
        functools.partial(_merge_body, d=d, vd=vd),
        grid=(n // tm,),
        in_specs=[row(d), _resident((1, d)), _resident(w_gate.shape), row(o_fox.shape[1]), row(o_hg.shape[1]),
                  row(hgate.shape[1]), _resident(o_gain.shape), _resident(gsum.shape),
                  _resident(w_pf.shape), _resident(w_ph.shape), _resident(w_out.shape)],
        out_specs=row(d),
        out_shape=jax.ShapeDtypeStruct((n, d), F32),
        compiler_params=_params("parallel"),
        name="merge",
    )(x, gain, w_gate, o_fox, o_hg, hgate, o_gain, gsum, w_pf, w_ph, w_out)


def _paged_body(pt_ref, *refs, g, ps, nh, dh, t):
    k_refs = refs[:g]
    v_refs = refs[g:2 * g]
    lf_refs = refs[2 * g:3 * g]
    (w_ref, q_ref, kn_ref, vn_ref, lfn_ref, o_ref,
     qbd_ref, ctcol_ref, carry_ref, m_ref, l_ref, acc_ref) = refs[3 * g:]
    j = pl.program_id(1)
    rows = nh * t
    fw = nh * dh
    row_head = lax.broadcasted_iota(jnp.int32, (rows, 1), 0) // t

    def new_token_cumsum():
        lfn = lfn_ref[...]
        sub = lax.broadcasted_iota(jnp.int32, lfn.shape, 0)
        ct = jnp.zeros_like(lfn)
        for s in range(t):
            ct = ct + jnp.where(sub >= s, lfn[s:s + 1, :], 0.0)
        return ct

    @pl.when(j == 0)
    def _():
        lane_head = lax.broadcasted_iota(jnp.int32, (rows, fw), 1) // dh
        q_rep = jnp.concatenate([q_ref[...]] * nh, axis=0)
        qbd_ref[...] = jnp.where(lane_head == row_head, q_rep, jnp.zeros_like(q_rep))
        ct_rep = jnp.concatenate([new_token_cumsum()] * nh, axis=0)
        lane = lax.broadcasted_iota(jnp.int32, ct_rep.shape, 1)
        ctcol_ref[...] = LOG2E * jnp.sum(jnp.where(lane == row_head, ct_rep, 0.0), axis=-1, keepdims=True)
        carry_ref[...] = jnp.zeros_like(carry_ref)
        m_ref[...] = jnp.full_like(m_ref, NEG_INF)
        l_ref[...] = jnp.zeros_like(l_ref)
        acc_ref[...] = jnp.zeros_like(acc_ref)

    qbd = qbd_ref[...]
    ctcol = ctcol_ref[...]

    def update(u, pv_fn):
        m_old = m_ref[...]
        m_new = jnp.maximum(m_old, jnp.max(u, axis=-1, keepdims=True) + ctcol)
        pe = jnp.exp2(u - (m_new - ctcol))
        alpha = jnp.exp2(m_old - m_new)
        l_ref[...] = alpha * l_ref[...] + jnp.sum(pe, axis=-1, keepdims=True)
        acc_ref[...] = alpha * acc_ref[...] + pv_fn(pe)
        m_ref[...] = m_new

    lf_all = jnp.concatenate([lf_refs[i][...] for i in range(g)], axis=0)
    sums = _dot01_r(lf_all, w_ref[...])
    carry = carry_ref[...]
    biases = [None] * g
    for i in reversed(range(g)):
        biases[i] = sums[i * nh:(i + 1) * nh, :ps] + carry
        carry = carry + sums[i * nh:(i + 1) * nh, ps:]
    carry_ref[...] = carry
    bias = LOG2E * jnp.concatenate(biases, axis=1)
    bias = jnp.broadcast_to(bias[:, None, :], (nh, t, g * ps)).reshape(rows, g * ps)

    s = jnp.concatenate([_dot(qbd, k_refs[i][...].astype(BF16)) for i in range(g)], axis=1)

    def pv_past(pe):
        out = None
        for i in range(g):
            part = _dot_nt(pe[:, i * ps:(i + 1) * ps].astype(BF16), v_refs[i][...].astype(BF16))
            out = part if out is None else out + part
        return out

    update(s + bias, pv_past)

    @pl.when(j == pl.num_programs(1) - 1)
    def _():
        pad = jnp.zeros((ps - t, fw), BF16)
        kn = jnp.concatenate([kn_ref[...], pad], axis=0)
        vn = jnp.concatenate([vn_ref[...], pad], axis=0)
        ct = new_token_cumsum()
        ct_pad = jnp.concatenate([ct, jnp.zeros((ps - t, ct.shape[1]), F32)], axis=0)
        lane = lax.broadcasted_iota(jnp.int32, (rows, ct.shape[1]), 1)
        onehot = jnp.where(lane == row_head, 1.0, 0.0).astype(BF16)
        c_s = _dot01_nt(onehot, ct_pad)
        col = lax.broadcasted_iota(jnp.int32, (rows, ps), 1)
        row_t = lax.broadcasted_iota(jnp.int32, (rows, ps), 0) % t
        u = jnp.where(col <= row_t, _dot_nt(qbd, kn) - LOG2E * c_s, NEG_INF)
        update(u, lambda pe: _dot(pe.astype(BF16), vn))
        o = acc_ref[...] / l_ref[...]
        lane_head = lax.broadcasted_iota(jnp.int32, (rows, fw), 1) // dh
        o = jnp.where(lane_head == row_head, o, 0.0)
        o_ref[...] = jnp.sum(o.reshape(nh, t, fw), axis=0).astype(o_ref.dtype)


def _attn_paged(cache_kt, cache_vt, cache_lft, page_table, q, k_new, v_new, lf_new, *, layer, t, dh):
    _, _, fw, ps = cache_kt.shape
    nb, n_pages = page_table.shape
    nh = fw // dh
    g = min(PAGES_PER_STEP, n_pages)
    ng = n_pages // g
    rows = nh * t
    assert n_pages % g == 0 and t <= ps and cache_lft.shape[2:] == (nh, ps)
    j_idx = np.arange(ps)
    w = np.concatenate([(j_idx[:, None] > j_idx[None, :]), np.ones((ps, ps), bool)], axis=1)
    w = jnp.asarray(w.astype(np.float32), BF16)

    def page(i, width):
        return pl.BlockSpec((None, None, width, ps),
                            lambda b, j, pt: (layer, pt[b, (ng - 1 - j) * g + i], 0, 0))

    new_rows = lambda width: pl.BlockSpec((t, width), lambda b, j, pt: (b, 0))
    grid_spec = pltpu.PrefetchScalarGridSpec(
        num_scalar_prefetch=1,
        grid=(nb, ng),
        in_specs=[page(i, fw) for i in range(g)] + [page(i, fw) for i in range(g)]
        + [page(i, nh) for i in range(g)]
        + [pl.BlockSpec(w.shape, lambda b, j, pt: (0, 0)),
           new_rows(fw), new_rows(fw), new_rows(fw), new_rows(lf_new.shape[1])],
        out_specs=new_rows(fw),
        scratch_shapes=[pltpu.VMEM((rows, fw), BF16), pltpu.VMEM((rows, 1), F32), pltpu.VMEM((nh, ps), F32),
                        pltpu.VMEM((rows, 1), F32), pltpu.VMEM((rows, 1), F32), pltpu.VMEM((rows, fw), F32)],
    )
    return pl.pallas_call(
        functools.partial(_paged_body, g=g, ps=ps, nh=nh, dh=dh, t=t),
        grid_spec=grid_spec,
        out_shape=jax.ShapeDtypeStruct((nb * t, fw), BF16),
        compiler_params=_params("parallel", "arbitrary"),
        name="fox_paged",
    )(page_table, *([cache_kt] * g), *([cache_vt] * g), *([cache_lft] * g), w, q, k_new, v_new, lf_new)


def _block_ones(width, group):
    i = np.arange(width) // group
    return jnp.asarray((i[:, None] == i[None, :]).astype(np.float32), BF16)


def kernel(x_prompt, x_sample, cache_k, cache_v, cache_logf, state_hgrn, page_table, ffn1_norm, ffn1_w_in, ffn1_w_out, mix_norm, w_in, fox_f_bias, fox_q_gain, fox_k_gain, hg_lb_logits, hg_o_gain, w_proj_fox, w_proj_hg, w_out, ffn2_norm, ffn2_w_in, ffn2_w_out):
    depth = w_in.shape[0]
    bp, lp, d = x_prompt.shape
    bs, ls, _ = x_sample.shape
    nh, dh = cache_k.shape[-2:]
    fw = nh * dh
    hg_heads, kd, vd = state_hgrn.shape[-3:]
    hw, hvw = hg_heads * kd, hg_heads * vd
    assert nh <= V7X_LANES and w_in.shape[-1] == 3 * fw + nh + 2 * hw + 2 * hvw + 2 * d

    gsum_fox = _block_ones(fw, dh)
    gsum_hg = _block_ones(hvw, vd)
    tc = min(ROW_TILE, lp)
    tril_seq = jnp.asarray(np.tril(np.ones((tc, tc), np.float32)), BF16)
    lb_logits = hg_lb_logits.astype(F32)
    cache_kt = jnp.transpose(cache_k, (0, 1, 3, 4, 2)).reshape(cache_k.shape[:2] + (fw, cache_k.shape[2]))
    cache_vt = jnp.transpose(cache_v, (0, 1, 3, 4, 2)).reshape(cache_v.shape[:2] + (fw, cache_v.shape[2]))
    cache_lft = jnp.transpose(cache_logf, (0, 1, 3, 2)).astype(F32)

    yp = x_prompt.reshape(bp * lp, d)
    ys = x_sample.reshape(bs * ls, d)
    outs = {name: [] for name in ("kp", "vp", "fp", "sp", "ks", "vs", "fs", "ss")}

    for l in range(depth):
        c_mix = 3 * fw + nh
        w_l = w_in[l]
        w_mix = jnp.concatenate(
            [w_l[:, :3 * fw], w_l[:, c_mix:c_mix + 2 * hw + 2 * hvw],
             jnp.pad(w_l[:, 3 * fw:c_mix], ((0, 0), (0, V7X_LANES - nh)))], axis=1).astype(BF16)
        w_gate = w_l[:, c_mix + 2 * hw + 2 * hvw:].astype(BF16)
        f_bias = jnp.pad(fox_f_bias[l], (0, V7X_LANES - nh)).reshape(1, V7X_LANES)
        q_gain = jnp.tile(fox_q_gain[l], nh).reshape(1, fw)
        k_gain = jnp.tile(fox_k_gain[l], nh).reshape(1, fw)
        o_gain = jnp.tile(hg_o_gain[l], hg_heads).reshape(1, hvw)
        mix_gain = mix_norm[l].reshape(1, d)
        w1_in, w1_out = ffn1_w_in[l].astype(BF16), ffn1_w_out[l].astype(BF16)
        w2_in, w2_out = ffn2_w_in[l].astype(BF16), ffn2_w_out[l].astype(BF16)
        w_pf, w_ph, w_o = w_proj_fox[l].astype(BF16), w_proj_hg[l].astype(BF16), w_out[l].astype(BF16)

        yp = _ffn(yp, ffn1_norm[l].reshape(1, d), w1_in, w1_out)
        ys = _ffn(ys, ffn1_norm[l].reshape(1, d), w1_in, w1_out)

        inproj = functools.partial(_inproj, gain=mix_gain, w_mix=w_mix, gsum=gsum_fox, q_gain=q_gain,
                                   k_gain=k_gain, f_bias=f_bias, lb_logits=lb_logits,
                                   fw=fw, hw=hw, hvw=hvw, dh=dh, layer=l)
        merge = functools.partial(_merge, gain=mix_gain, w_gate=w_gate, o_gain=o_gain, gsum=gsum_hg,
                                  w_pf=w_pf, w_ph=w_ph, w_out=w_o, vd=vd)

        q, kf, vf, kb, vb, lf, hq, hk, hlf, hv, hgate = inproj(yp)
        c, kaug = _cumsum(lf.reshape(bp, lp, V7X_LANES), kb, tril_seq, nh=nh, dh=dh)
        c_row = jnp.transpose(c[:, :, :nh], (0, 2, 1))
        v_t = jnp.transpose(vb.reshape(bp, lp, fw), (0, 2, 1))
        o_fox = _attn_prompt(q, kaug, v_t, c_row, b=bp, l=lp, dh=dh)
        s0_t = jnp.zeros((bp, hg_heads, vd, kd), F32)
        o_hg, s_t = _gla(hq, hk, hlf, hv, s0_t, b=bp, l=lp)
        yp = merge(yp, o_fox=o_fox, o_hg=o_hg, hgate=hgate)
        outs["kp"].append(kf.reshape(bp, lp, nh, dh))
        outs["vp"].append(vf.reshape(bp, lp, nh, dh))
        outs["fp"].append(lf[:, :nh].reshape(bp, lp, nh).astype(cache_logf.dtype))
        outs["sp"].append(jnp.swapaxes(s_t, -1, -2).astype(x_prompt.dtype))

        q, kf, vf, kb, vb, lf, hq, hk, hlf, hv, hgate = inproj(ys)
        o_fox = _attn_paged(cache_kt, cache_vt, cache_lft, page_table, q, kb, vb, lf, layer=l, t=ls, dh=dh)
        s0_t = jnp.swapaxes(state_hgrn[l].astype(F32), -1, -2)
        o_hg, s_t = _gla(hq, hk, hlf, hv, s0_t, b=bs, l=ls)
        ys = merge(ys, o_fox=o_fox, o_hg=o_hg, hgate=hgate)
        outs["ks"].append(kf.reshape(bs, ls, nh, dh))
        outs["vs"].append(vf.reshape(bs, ls, nh, dh))
        outs["fs"].append(lf[:, :nh].reshape(bs, ls, nh).astype(cache_logf.dtype))
        outs["ss"].append(jnp.swapaxes(s_t, -1, -2).astype(state_hgrn.dtype))

        yp = _ffn(yp, ffn2_norm[l].reshape(1, d), w2_in, w2_out)
        ys = _ffn(ys, ffn2_norm[l].reshape(1, d), w2_in, w2_out)

    st = lambda name: jnp.stack(outs[name])
    return (yp.reshape(bp, lp, d), ys.reshape(bs, ls, d), st("kp"), st("vp"), st("fp"), st("sp"),
            st("ks"), st("vs"), st("fs"), st("ss"))
```

```python
import functools

import jax
import jax.numpy as jnp
import numpy as np
from jax import lax
from jax.experimental import pallas as pl
from jax.experimental.pallas import tpu as pltpu

F32 = jnp.float32
BF16 = jnp.bfloat16
EPS = 1e-6
NEG_INF = float("-inf")
LOG2E = 1.4426950408889634

V7X_LANES = 128
V7X_SUBLANES = 8
V7X_VMEM_BYTES = 64 * 1024 * 1024
VMEM_LIMIT = (V7X_VMEM_BYTES * 3) // 4

ROW_TILE = 512
ATTN_TILE = 512
ATTN_UNROLL = 2
GLA_CHUNK = 64
GLA_LEAF = V7X_SUBLANES
GLA_ROWS = 512
PAGES_PER_STEP = 32


def _params(*sem):
    return pltpu.CompilerParams(dimension_semantics=sem, vmem_limit_bytes=VMEM_LIMIT)


def _resident(shape):
    nd = len(shape)
    return pl.BlockSpec(shape, lambda *_: (0,) * nd, pipeline_mode=pl.Buffered(1))


def _dot(a, b):
    return jnp.dot(a, b, preferred_element_type=F32)


def _dot_nt(a, b):
    return lax.dot_general(a, b, (((1,), (1,)), ((), ())), preferred_element_type=F32)


def _dot_tn(a, b):
    return lax.dot_general(a, b, (((0,), (0,)), ((), ())), preferred_element_type=F32)


def _split3(x):
    hi = x.astype(BF16)
    r1 = x - hi.astype(F32)
    mid = r1.astype(BF16)
    lo = (r1 - mid.astype(F32)).astype(BF16)
    return hi, mid, lo


def _dot01(m01, x):
    hi, mid, lo = _split3(x)
    return _dot(m01, hi) + _dot(m01, mid) + _dot(m01, lo)


def _dot01_nt(m01, x):
    hi, mid, lo = _split3(x)
    return _dot_nt(m01, hi) + _dot_nt(m01, mid) + _dot_nt(m01, lo)


def _dot01_r(x, m01):
    hi, mid, lo = _split3(x)
    return _dot(hi, m01) + _dot(mid, m01) + _dot(lo, m01)


def _rms_rows(x, gain):
    return x * lax.rsqrt(jnp.mean(x * x, axis=-1, keepdims=True) + EPS) * gain


def _sigmoid(x):
    return 1.0 / (1.0 + jnp.exp(-x))


def _ffn_body(x_ref, g_ref, wi_ref, wo_ref, o_ref, acc_ref, *, d_ff, tf):
    x = x_ref[...]
    h = _rms_rows(x, g_ref[...]).astype(BF16)
    for j in range(d_ff // tf):
        a = _dot(h, wi_ref[:, j * tf:(j + 1) * tf])
        b = _dot(h, wi_ref[:, d_ff + j * tf:d_ff + (j + 1) * tf])
        g = (a * _sigmoid(a) * b).astype(BF16)
        y = _dot(g, wo_ref[j * tf:(j + 1) * tf, :])
        if j == 0:
            acc_ref[...] = y
        else:
            acc_ref[...] += y
    o_ref[...] = x + 0.5 * acc_ref[...]


def _ffn(x, gain, w_in, w_out):
    n, d = x.shape
    d_ff = w_out.shape[0]
    tm = min(ROW_TILE, n)
    tf = 2 * V7X_LANES if d_ff % (2 * V7X_LANES) == 0 else V7X_LANES
    assert n % tm == 0 and d_ff % tf == 0
    return pl.pallas_call(
        functools.partial(_ffn_body, d_ff=d_ff, tf=tf),
        grid=(n // tm,),
        in_specs=[pl.BlockSpec((tm, d), lambda i: (i, 0)),
                  _resident((1, d)),
                  _resident((d, 2 * d_ff)),
                  _resident((d_ff, d))],
        out_specs=pl.BlockSpec((tm, d), lambda i: (i, 0)),
        out_shape=jax.ShapeDtypeStruct((n, d), F32),
        scratch_shapes=[pltpu.VMEM((tm, d), F32)],
        compiler_params=_params("parallel"),
        name="ffn",
    )(x, gain, w_in, w_out)


def _inproj_body(x_ref, g_ref, w_ref, wvt_ref, gsum_ref, qg_ref, kg_ref, fb_ref, lbl_ref,
                 q_ref, kf_ref, vf_ref, kb_ref, vb_ref, lf_ref,
                 hq_ref, hk_ref, hlf_ref, hv_ref, hg_ref, *, fw, hw, hvw, dh, layer, seq_major):
    h = _rms_rows(x_ref[...], g_ref[...]).astype(BF16)

    def proj(c0, n):
        return _dot(h, w_ref[:, c0:c0 + n])

    gsum = gsum_ref[...]

    def head_norm(z, gain):
        ms = _dot((z * z).astype(BF16), gsum) * (1.0 / dh)
        return z * lax.rsqrt(ms + EPS) * gain

    q = head_norm(proj(0, fw), qg_ref[...])
    q_ref[...] = (q * (dh ** -0.5 * LOG2E)).astype(BF16)
    k = head_norm(proj(fw, fw), kg_ref[...])
    kb_ref[...] = k.astype(BF16)
    if seq_major:
        v_t = _dot_nt(wvt_ref[...], h)
        kf_ref[...] = k.T
        vf_ref[...] = v_t
        vb_ref[...] = v_t.astype(BF16)
    else:
        v = proj(2 * fw, fw)
        kf_ref[...] = k
        vf_ref[...] = v
        vb_ref[...] = v.astype(BF16)

    c0 = 3 * fw
    hq = proj(c0, hw)
    hq_ref[...] = hq * _sigmoid(hq)

    lg = lbl_ref[...]
    e = jnp.exp(lg - jnp.max(lg, axis=0, keepdims=True))
    sm = e / jnp.sum(e, axis=0, keepdims=True)
    lb = jnp.zeros_like(sm[0:1])
    for i in range(1, layer + 1):
        lb = lb + sm[i:i + 1]
    log_lb = jnp.log(lb)
    log_1m_lb = jnp.log1p(-lb)

    z = proj(c0 + hw, hw)
    ez = jnp.exp(-jnp.abs(z))
    log_sig = jnp.minimum(z, 0.0) - jnp.log1p(ez)
    t = log_1m_lb + log_sig
    mx = jnp.maximum(log_lb, t)
    hlf_ref[...] = mx + jnp.log1p(jnp.exp(-jnp.abs(log_lb - t)))
    hk_ref[...] = (1.0 - lb) * (jnp.where(z >= 0.0, ez, 1.0) / (1.0 + ez))

    hv_ref[...] = proj(c0 + 2 * hw, hvw).astype(BF16)
    hg_ref[...] = proj(c0 + 2 * hw + hvw, hvw)

    zf = proj(c0 + 2 * hw + 2 * hvw, V7X_LANES) + fb_ref[...]
    lf_ref[...] = jnp.minimum(zf, 0.0) - jnp.log1p(jnp.exp(-jnp.abs(zf)))


def _inproj(x, gain, w_mix, w_vt, gsum, q_gain, k_gain, f_bias, lb_logits, *, fw, hw, hvw, dh, layer, seqs=None):
    n, d = x.shape
    tm = min(ROW_TILE, n)
    assert n % tm == 0
    row = lambda w: pl.BlockSpec((tm, w), lambda i: (i, 0))
    sds = lambda w, dt: jax.ShapeDtypeStruct((n, w), dt)
    if seqs is None:
        kv_specs = [row(fw), row(fw), row(fw)]
        kv_shapes = [sds(fw, F32), sds(fw, F32), sds(fw, BF16)]
    else:
        b, l = seqs
        nt = l // tm
        assert b * l == n and l % tm == 0
        kv_specs = [pl.BlockSpec((None, fw, tm), lambda i: (i // nt, 0, i % nt))] * 3
        kv_shapes = [jax.ShapeDtypeStruct((b, fw, l), dt) for dt in (F32, F32, BF16)]
    return pl.pallas_call(
        functools.partial(_inproj_body, fw=fw, hw=hw, hvw=hvw, dh=dh, layer=layer, seq_major=seqs is not None),
        grid=(n // tm,),
        in_specs=[row(d), _resident((1, d)), _resident(w_mix.shape), _resident(w_vt.shape), _resident(gsum.shape),
                  _resident((1, fw)), _resident((1, fw)), _resident((1, V7X_LANES)),
                  _resident(lb_logits.shape)],
        out_specs=[row(fw), kv_specs[0], kv_specs[1], row(fw), kv_specs[2], row(V7X_LANES),
                   row(hw), row(hw), row(hw), row(hvw), row(hvw)],
        out_shape=[sds(fw, BF16), kv_shapes[0], kv_shapes[1], sds(fw, BF16), kv_shapes[2],
                   sds(V7X_LANES, F32),
                   sds(hw, F32), sds(hw, F32), sds(hw, F32), sds(hvw, BF16), sds(hvw, F32)],
        compiler_params=_params("parallel"),
        name="inproj",
    )(x, gain, w_mix, w_vt, gsum, q_gain, k_gain, f_bias, lb_logits)


BIAS_PIECES = 3


def _bias_lane(hh, dh):
    return ((hh + 1) % (V7X_LANES // dh)) * dh


def _cumsum_body(lf_ref, k_ref, tril_ref, place_ref, c_ref, kaug_ref, carry_ref, *, nh, dh):
    @pl.when(pl.program_id(1) == 0)
    def _():
        carry_ref[...] = jnp.zeros_like(carry_ref)

    c = _dot01(tril_ref[...], lf_ref[...]) + carry_ref[...]
    c_ref[...] = c
    carry_ref[...] = c[-1:, :]
    pieces = jnp.concatenate(_split3(-LOG2E * c), axis=1)
    lane = lax.broadcasted_iota(jnp.int32, (1, V7X_LANES), 1)
    hp = V7X_LANES // dh
    for h in range(nh):
        tile = h // hp
        placed = _dot(pieces, place_ref[h]).astype(BF16)
        keys = k_ref[:, tile * V7X_LANES:(tile + 1) * V7X_LANES]
        kaug_ref[h] = jnp.where((lane // dh) == (h % hp), keys, placed)


def _cumsum(lf, k, tril, *, nh, dh):
    b, l, w = lf.shape
    tc = tril.shape[0]
    nt = l // tc
    place = np.zeros((nh, BIAS_PIECES * w, V7X_LANES), np.float32)
    for h in range(nh):
        for piece in range(BIAS_PIECES):
            place[h, piece * w + h, _bias_lane(h % (V7X_LANES // dh), dh) + piece] = 1.0
    place = jnp.asarray(place, BF16)
    return pl.pallas_call(
        functools.partial(_cumsum_body, nh=nh, dh=dh),
        grid=(b, nt),
        in_specs=[pl.BlockSpec((None, tc, w), lambda i, j: (i, j, 0)),
                  pl.BlockSpec((tc, k.shape[1]), lambda i, j: (i * nt + j, 0)),
                  _resident(tril.shape), _resident(place.shape)],
        out_specs=[pl.BlockSpec((None, tc, w), lambda i, j: (i, j, 0)),
                   pl.BlockSpec((None, nh, tc, V7X_LANES), lambda i, j: (i, 0, j, 0))],
        out_shape=[jax.ShapeDtypeStruct((b, l, w), F32), jax.ShapeDtypeStruct((b, nh, l, V7X_LANES), BF16)],
        scratch_shapes=[pltpu.VMEM((1, w), F32)],
        compiler_params=_params("parallel", "arbitrary"),
        name="logf_cumsum",
    )(lf, k, tril, place)


def _attn_body(q_ref, kaug_ref, vt_ref, crow_ref, o_ref, m_ref, acc_ref, *, t, dh):
    p = pl.program_id(1)
    qi = pl.program_id(2)
    hp = V7X_LANES // dh
    q = q_ref[...]
    lane = lax.broadcasted_iota(jnp.int32, (1, V7X_LANES), 1)
    key_i = lax.broadcasted_iota(jnp.int32, (t, t), 0)
    qry_i = lax.broadcasted_iota(jnp.int32, (t, t), 1)
    feat = lax.broadcasted_iota(jnp.int32, (V7X_LANES, t), 0)
    qas, cts, keep, ones = [], [], [], []
    for hh in range(hp):
        b0 = _bias_lane(hh, dh)
        one_lanes = jnp.where((lane >= b0) & (lane < b0 + BIAS_PIECES), 1.0, 0.0).astype(BF16)
        qas.append(jnp.where((lane // dh) == hh, q, one_lanes))
        cts.append(LOG2E * crow_ref[pl.ds(p * hp + hh, 1), :])
        keep.append(jnp.where((feat // dh) == hh, 1.0, 0.0).astype(BF16))
        ones.append(jnp.where(feat == b0, 1.0, 0.0).astype(BF16))
        m_ref[hh] = jnp.full(m_ref.shape[1:], NEG_INF, F32)
        acc_ref[hh] = jnp.zeros(acc_ref.shape[1:], F32)

    def steps(kbs, diagonal_last):
        r0s = [pl.multiple_of(kb * t, t) for kb in kbs]
        us = [[_dot_nt(kaug_ref[hh, pl.ds(r0, t), :], qas[hh]) for hh in range(hp)] for r0 in r0s]
        heads = range(hp)
        for n, (r0, u_heads) in enumerate(zip(r0s, us)):
            vt = vt_ref[:, pl.ds(r0, t)]
            if diagonal_last and n == len(kbs) - 1:
                u_heads = [jnp.where(key_i <= qry_i, u, NEG_INF) for u in u_heads]
            m_old = [m_ref[hh] for hh in heads]
            m_new = [jnp.maximum(m_old[hh], jnp.max(u_heads[hh], axis=0, keepdims=True) + cts[hh]) for hh in heads]
            pe = [jnp.exp2(u_heads[hh] - (m_new[hh] - cts[hh])).astype(BF16) for hh in heads]
            pv = [_dot(vt * keep[hh] + ones[hh], pe[hh]) for hh in heads]
            for hh in heads:
                acc_ref[hh] = jnp.exp2(m_old[hh] - m_new[hh]) * acc_ref[hh] + pv[hh]
                m_ref[hh] = m_new[hh]

    def body(i, carry):
        steps([ATTN_UNROLL * i + r for r in range(ATTN_UNROLL)], False)
        return carry

    lax.fori_loop(0, qi // ATTN_UNROLL, body, 0)
    for rem in range(ATTN_UNROLL):
        @pl.when(qi % ATTN_UNROLL == rem)
        def _(rem=rem):
            steps([qi - rem + r for r in range(rem)] + [qi], True)
    o_t = None
    for hh in range(hp):
        b0 = _bias_lane(hh, dh)
        acc = acc_ref[hh]
        o_h = acc / acc[b0:b0 + 1, :]
        o_t = o_h if o_t is None else jnp.where((feat // dh) == hh, o_h, o_t)
    o_ref[...] = o_t.T.astype(o_ref.dtype)


def _attn_prompt(q, kaug, vt, crow, *, b, l, dh):
    n, fw = q.shape
    t = min(ATTN_TILE, l)
    nq = l // t
    hp = V7X_LANES // dh
    assert l % t == 0 and fw % V7X_LANES == 0 and V7X_LANES % dh == 0 and BIAS_PIECES < dh
    return pl.pallas_call(
        functools.partial(_attn_body, t=t, dh=dh),
        grid=(b, fw // V7X_LANES, nq),
        in_specs=[pl.BlockSpec((t, V7X_LANES), lambda i, p, j: (i * nq + j, p)),
                  pl.BlockSpec((None, hp, l, V7X_LANES), lambda i, p, j: (i, p, 0, 0)),
                  pl.BlockSpec((None, V7X_LANES, l), lambda i, p, j: (i, p, 0)),
                  pl.BlockSpec((None, crow.shape[1], t), lambda i, p, j: (i, 0, j))],
        out_specs=pl.BlockSpec((t, V7X_LANES), lambda i, p, j: (i * nq + j, p)),
        out_shape=jax.ShapeDtypeStruct((n, fw), BF16),
        scratch_shapes=[pltpu.VMEM((hp, 1, t), F32), pltpu.VMEM((hp, V7X_LANES, t), F32)],
        compiler_params=_params("parallel", "parallel", "arbitrary"),
        name="fox_prompt",
    )(q, kaug, vt, crow)


def _gla_masks(c):
    r = np.arange(c)[:, None]
    s = np.arange(c)[None, :]
    tril = (s <= r).astype(np.float32)
    levels = []
    m = c // 2
    while m >= GLA_LEAF:
        levels.append((((r ^ s) < 2 * m) & ((r & m) != 0) & ((s & m) == 0)).astype(np.float32))
        m //= 2
    if not levels:
        levels.append(np.zeros((c, c), np.float32))
    return jnp.asarray(tril, BF16), jnp.asarray(np.stack(levels), F32)


def _gla_chunk(q, k, lf, v, states, tril, lvl_ref, leaf_refs, *, c, nh, kd, vd):
    def head(x, h, w):
        return x[:, h * w:(h + 1) * w]

    b = _dot01(tril, LOG2E * lf)
    q = q * (kd ** -0.5)
    b_last = b[c - 1:c, :]
    q_hat = (q * jnp.exp2(b)).astype(BF16)
    k_hat = (k * jnp.exp2(b_last - b)).astype(BF16)
    decay_last = jnp.exp2(b_last)
    outs = [_dot_nt(head(q_hat, h, kd), states[h].astype(BF16)) for h in range(nh)]
    new_states = [states[h] * head(decay_last, h, kd) + _dot_tn(head(v, h, vd), head(k_hat, h, kd))
                  for h in range(nh)]

    m = c // 2
    lvl = 0
    scores = [None] * nh
    while m >= GLA_LEAF:
        pieces = []
        for blk in range(c // (2 * m)):
            mid = blk * 2 * m + m
            pieces.append(jnp.broadcast_to(b[mid - 1:mid, :], (2 * m, b.shape[1])))
        ref = pieces[0] if len(pieces) == 1 else jnp.concatenate(pieces, axis=0)
        e = jnp.exp2(-jnp.abs(b - ref))
        qe = (q * e).astype(BF16)
        ke = (k * e).astype(BF16)
        mask = lvl_ref[lvl]
        for h in range(nh):
            part = mask * _dot_nt(head(qe, h, kd), head(ke, h, kd))
            scores[h] = part if scores[h] is None else scores[h] + part
        m //= 2
        lvl += 1
    if scores[0] is not None:
        outs = [outs[h] + _dot(scores[h].astype(BF16), head(v, h, vd)) for h in range(nh)]

    nl = c // GLA_LEAF
    b_ref, q_ref, k_ref, v_ref, o_ref = leaf_refs
    v32 = v.astype(F32)
    for h in range(nh):
        b_ref[h] = head(b, h, kd)
        q_ref[h] = head(q, h, kd)
        k_ref[h] = head(k, h, kd)
        v_ref[h] = head(v32, h, vd)

    def rows(ref, i):
        return jnp.concatenate([ref[h, pl.ds(i, nl, stride=GLA_LEAF), :] for h in range(nh)], axis=1)

    bs = [rows(b_ref, i) for i in range(GLA_LEAF)]
    ks = [rows(k_ref, i) for i in range(GLA_LEAF)]
    vs = [rows(v_ref, i) for i in range(GLA_LEAF)]
    for t in range(GLA_LEAF):
        q_t = rows(q_ref, t)
        o_t = [None] * nh
        for s in range(t + 1):
            prod = jnp.exp2(bs[t] - bs[s]) * q_t * ks[s]
            for h in range(nh):
                w = jnp.sum(head(prod, h, kd), axis=-1, keepdims=True)
                term = w * head(vs[s], h, vd)
                o_t[h] = term if o_t[h] is None else o_t[h] + term
        for h in range(nh):
            o_ref[h, pl.ds(t, nl, stride=GLA_LEAF), :] = o_t[h]
    return jnp.concatenate([outs[h] + o_ref[h] for h in range(nh)], axis=1), new_states


def _gla_body(q_ref, k_ref, lf_ref, v_ref, s0_ref, tril_ref, lvl_ref, o_ref, sT_ref, st_ref, *leaf_refs,
              c, rows, nh, kd, vd):
    j = pl.program_id(1)

    @pl.when(j == 0)
    def _():
        st_ref[...] = s0_ref[...]

    tril = tril_ref[...]

    def chunk(ci, carry):
        rs = pl.ds(pl.multiple_of(ci * c, c), c)
        o, new_states = _gla_chunk(q_ref[rs, :], k_ref[rs, :], lf_ref[rs, :], v_ref[rs, :],
                                   [st_ref[h] for h in range(nh)], tril, lvl_ref, leaf_refs,
                                   c=c, nh=nh, kd=kd, vd=vd)
        o_ref[rs, :] = o
        for h in range(nh):
            st_ref[h] = new_states[h]
        return carry

    if rows == c:
        chunk(0, 0)
    else:
        lax.fori_loop(0, rows // c, chunk, 0, unroll=2)

    @pl.when(j == pl.num_programs(1) - 1)
    def _():
        sT_ref[...] = st_ref[...]


def _gla(q, k, lf, v, s0_t, *, b, l):
    n, hw = q.shape
    hvw = v.shape[1]
    nh, vd, kd = s0_t.shape[1:]
    c = min(GLA_CHUNK, l)
    rows = min(GLA_ROWS, l)
    assert l % rows == 0 and rows % c == 0 and c % GLA_LEAF == 0
    tril, lvls = _gla_masks(c)
    nr = l // rows
    row = lambda w: pl.BlockSpec((rows, w), lambda i, j: (i * nr + j, 0))
    state = pl.BlockSpec((None, nh, vd, kd), lambda i, j: (i, 0, 0, 0))
    return pl.pallas_call(
        functools.partial(_gla_body, c=c, rows=rows, nh=nh, kd=kd, vd=vd),
        grid=(b, nr),
        in_specs=[row(hw), row(hw), row(hw), row(hvw), state, _resident(tril.shape), _resident(lvls.shape)],
        out_specs=[row(hvw), state],
        out_shape=[jax.ShapeDtypeStruct((n, hvw), F32), jax.ShapeDtypeStruct(s0_t.shape, F32)],
        scratch_shapes=[pltpu.VMEM((nh, vd, kd), F32)]
        + [pltpu.VMEM((nh, c, w), F32) for w in (kd, kd, kd, vd, vd)],
        compiler_params=_params("parallel", "arbitrary"),
        name="hgrn2",
    )(q, k, lf, v, s0_t, tril, lvls)


def _merge_body(x_ref, g_ref, wg_ref, of_ref, oh_ref, hg_ref, og_ref, gsum_ref, wpf_ref, wph_ref, wo_ref,
                o_ref, *, d, vd):
    x = x_ref[...]
    h = _rms_rows(x, g_ref[...]).astype(BF16)
    gate_fox = _sigmoid(_dot(h, wg_ref[:, :d]))
    gate_hg = _sigmoid(_dot(h, wg_ref[:, d:]))
    y_fox = _dot(of_ref[...], wpf_ref[...])
    oh = oh_ref[...]
    ms = _dot((oh * oh).astype(BF16), gsum_ref[...]) * (1.0 / vd)
    hg = hg_ref[...]
    ob = (oh * lax.rsqrt(ms + EPS) * og_ref[...]) * (hg * _sigmoid(hg))
    y_hg = _dot(ob.astype(BF16), wph_ref[...])
    y = gate_fox * y_fox + gate_hg * y_hg
    o_ref[...] = x + _dot(y.astype(BF16), wo_ref[...])


def _merge(x, gain, w_gate, o_fox, o_hg, hgate, o_gain, gsum, w_pf, w_ph, w_out, *, vd):
    n, d = x.shape
    tm = min(ROW_TILE, n)
    assert n % tm == 0
    row = lambda w: pl.BlockSpec((tm, w), lambda i: (i, 0))
    return pl.pallas_call(
        functools.partial(_merge_body, d=d, vd=vd),
        grid=(n // tm,),
        in_specs=[row(d), _resident((1, d)), _resident(w_gate.shape), row(o_fox.shape[1]), row(o_hg.shape[1]),
                  row(hgate.shape[1]), _resident(o_gain.shape), _resident(gsum.shape),
                  _resident(w_pf.shape), _resident(w_ph.shape), _resident(w_out.shape)],
        out_specs=row(d),
        out_shape=jax.ShapeDtypeStruct((n, d), F32),
        compiler_params=_params("parallel"),
        name="merge",
    )(x, gain, w_gate, o_fox, o_hg, hgate, o_gain, gsum, w_pf, w_ph, w_out)


def _paged_body(pt_ref, *refs, g, ps, nh, dh, t):
    k_refs = refs[:g]
    v_refs = refs[g:2 * g]
    lf_refs = refs[2 * g:3 * g]
    (w_ref, q_ref, kn_ref, vn_ref, lfn_ref, o_ref,
     qbd_ref, ctcol_ref, carry_ref, m_ref, l_ref, acc_ref) = refs[3 * g:]
    j = pl.program_id(1)
    rows = nh * t
    fw = nh * dh
    row_head = lax.broadcasted_iota(jnp.int32, (rows, 1), 0) // t

    def new_token_cumsum():
        lfn = lfn_ref[...]
        sub = lax.broadcasted_iota(jnp.int32, lfn.shape, 0)
        ct = jnp.zeros_like(lfn)
        for s in range(t):
            ct = ct + jnp.where(sub >= s, lfn[s:s + 1, :], 0.0)
        return ct

    @pl.when(j == 0)
    def _():
        lane_head = lax.broadcasted_iota(jnp.int32, (rows, fw), 1) // dh
        q_rep = jnp.concatenate([q_ref[...]] * nh, axis=0)
        qbd_ref[...] = jnp.where(lane_head == row_head, q_rep, jnp.zeros_like(q_rep))
        ct_rep = jnp.concatenate([new_token_cumsum()] * nh, axis=0)
        lane = lax.broadcasted_iota(jnp.int32, ct_rep.shape, 1)
        ctcol_ref[...] = LOG2E * jnp.sum(jnp.where(lane == row_head, ct_rep, 0.0), axis=-1, keepdims=True)
        carry_ref[...] = jnp.zeros_like(carry_ref)
        m_ref[...] = jnp.full_like(m_ref, NEG_INF)
        l_ref[...] = jnp.zeros_like(l_ref)
        acc_ref[...] = jnp.zeros_like(acc_ref)

    qbd = qbd_ref[...]
    ctcol = ctcol_ref[...]

    def update(u, pv_fn):
        m_old = m_ref[...]
        m_new = jnp.maximum(m_old, jnp.max(u, axis=-1, keepdims=True) + ctcol)
        pe = jnp.exp2(u - (m_new - ctcol))
        alpha = jnp.exp2(m_old - m_new)
        l_ref[...] = alpha * l_ref[...] + jnp.sum(pe, axis=-1, keepdims=True)
        acc_ref[...] = alpha * acc_ref[...] + pv_fn(pe)
        m_ref[...] = m_new

    lf_all = jnp.concatenate([lf_refs[i][...] for i in range(g)], axis=0)
    sums = _dot01_r(lf_all, w_ref[...])
    carry = carry_ref[...]
    biases = [None] * g
    for i in reversed(range(g)):
        biases[i] = sums[i * nh:(i + 1) * nh, :ps] + carry
        carry = carry + sums[i * nh:(i + 1) * nh, ps:]
    carry_ref[...] = carry
    bias = LOG2E * jnp.concatenate(biases, axis=1)
    bias = jnp.broadcast_to(bias[:, None, :], (nh, t, g * ps)).reshape(rows, g * ps)

    s = jnp.concatenate([_dot(qbd, k_refs[i][...].astype(BF16)) for i in range(g)], axis=1)

    def pv_past(pe):
        out = None
        for i in range(g):
            part = _dot_nt(pe[:, i * ps:(i + 1) * ps].astype(BF16), v_refs[i][...].astype(BF16))
            out = part if out is None else out + part
        return out

    update(s + bias, pv_past)

    @pl.when(j == pl.num_programs(1) - 1)
    def _():
        pad = jnp.zeros((ps - t, fw), BF16)
        kn = jnp.concatenate([kn_ref[...], pad], axis=0)
        vn = jnp.concatenate([vn_ref[...], pad], axis=0)
        ct = new_token_cumsum()
        ct_pad = jnp.concatenate([ct, jnp.zeros((ps - t, ct.shape[1]), F32)], axis=0)
        lane = lax.broadcasted_iota(jnp.int32, (rows, ct.shape[1]), 1)
        onehot = jnp.where(lane == row_head, 1.0, 0.0).astype(BF16)
        c_s = _dot01_nt(onehot, ct_pad)
        col = lax.broadcasted_iota(jnp.int32, (rows, ps), 1)
        row_t = lax.broadcasted_iota(jnp.int32, (rows, ps), 0) % t
        u = jnp.where(col <= row_t, _dot_nt(qbd, kn) - LOG2E * c_s, NEG_INF)
        update(u, lambda pe: _dot(pe.astype(BF16), vn))
        o = acc_ref[...] / l_ref[...]
        lane_head = lax.broadcasted_iota(jnp.int32, (rows, fw), 1) // dh
        o = jnp.where(lane_head == row_head, o, 0.0)
        o_ref[...] = jnp.sum(o.reshape(nh, t, fw), axis=0).astype(o_ref.dtype)


def _attn_paged(cache_kt, cache_vt, cache_lft, page_table, q, k_new, v_new, lf_new, *, layer, t, dh):
    _, _, fw, ps = cache_kt.shape
    nb, n_pages = page_table.shape
    nh = fw // dh
    g = min(PAGES_PER_STEP, n_pages)
    ng = n_pages // g
    rows = nh * t
    assert n_pages % g == 0 and t <= ps and cache_lft.shape[2:] == (nh, ps)
    j_idx = np.arange(ps)
    w = np.concatenate([(j_idx[:, None] > j_idx[None, :]), np.ones((ps, ps), bool)], axis=1)
    w = jnp.asarray(w.astype(np.float32), BF16)

    def page(i, width):
        return pl.BlockSpec((None, None, width, ps),
                            lambda b, j, pt: (layer, pt[b, (ng - 1 - j) * g + i], 0, 0))

    new_rows = lambda width: pl.BlockSpec((t, width), lambda b, j, pt: (b, 0))
    grid_spec = pltpu.PrefetchScalarGridSpec(
        num_scalar_prefetch=1,
        grid=(nb, ng),
        in_specs=[page(i, fw) for i in range(g)] + [page(i, fw) for i in range(g)]
        + [page(i, nh) for i in range(g)]
        + [pl.BlockSpec(w.shape, lambda b, j, pt: (0, 0)),
           new_rows(fw), new_rows(fw), new_rows(fw), new_rows(lf_new.shape[1])],
        out_specs=new_rows(fw),
        scratch_shapes=[pltpu.VMEM((rows, fw), BF16), pltpu.VMEM((rows, 1), F32), pltpu.VMEM((nh, ps), F32),
                        pltpu.VMEM((rows, 1), F32), pltpu.VMEM((rows, 1), F32), pltpu.VMEM((rows, fw), F32)],
    )
    return pl.pallas_call(
        functools.partial(_paged_body, g=g, ps=ps, nh=nh, dh=dh, t=t),
        grid_spec=grid_spec,
        out_shape=jax.ShapeDtypeStruct((nb * t, fw), BF16),
        compiler_params=_params("parallel", "arbitrary"),
        name="fox_paged",
    )(page_table, *([cache_kt] * g), *([cache_vt] * g), *([cache_lft] * g), w, q, k_new, v_new, lf_new)


def _block_ones(width, group):
    i = np.arange(width) // group
    return jnp.asarray((i[:, None] == i[None, :]).astype(np.float32), BF16)


def kernel(x_prompt, x_sample, cache_k, cache_v, cache_logf, state_hgrn, page_table, ffn1_norm, ffn1_w_in, ffn1_w_out, mix_norm, w_in, fox_f_bias, fox_q_gain, fox_k_gain, hg_lb_logits, hg_o_gain, w_proj_fox, w_proj_hg, w_out, ffn2_norm, ffn2_w_in, ffn2_w_out):
    depth = w_in.shape[0]
    bp, lp, d = x_prompt.shape
    bs, ls, _ = x_sample.shape
    nh, dh = cache_k.shape[-2:]
    fw = nh * dh
    hg_heads, kd, vd = state_hgrn.shape[-3:]
    hw, hvw = hg_heads * kd, hg_heads * vd
    assert nh <= V7X_LANES and w_in.shape[-1] == 3 * fw + nh + 2 * hw + 2 * hvw + 2 * d

    gsum_fox = _block_ones(fw, dh)
    gsum_hg = _block_ones(hvw, vd)
    tc = min(ROW_TILE, lp)
    tril_seq = jnp.asarray(np.tril(np.ones((tc, tc), np.float32)), BF16)
    lb_logits = hg_lb_logits.astype(F32)
    cache_kt = jnp.transpose(cache_k, (0, 1, 3, 4, 2)).reshape(cache_k.shape[:2] + (fw, cache_k.shape[2]))
    cache_vt = jnp.transpose(cache_v, (0, 1, 3, 4, 2)).reshape(cache_v.shape[:2] + (fw, cache_v.shape[2]))
    cache_lft = jnp.transpose(cache_logf, (0, 1, 3, 2)).astype(F32)

    yp = x_prompt.reshape(bp * lp, d)
    ys = x_sample.reshape(bs * ls, d)
    outs = {name: [] for name in ("kp", "vp", "fp", "sp", "ks", "vs", "fs", "ss")}

    for l in range(depth):
        c_mix = 3 * fw + nh
        w_l = w_in[l]
        w_mix = jnp.concatenate(
            [w_l[:, :3 * fw], w_l[:, c_mix:c_mix + 2 * hw + 2 * hvw],
             jnp.pad(w_l[:, 3 * fw:c_mix], ((0, 0), (0, V7X_LANES - nh)))], axis=1).astype(BF16)
        w_gate = w_l[:, c_mix + 2 * hw + 2 * hvw:].astype(BF16)
        f_bias = jnp.pad(fox_f_bias[l], (0, V7X_LANES - nh)).reshape(1, V7X_LANES)
        q_gain = jnp.tile(fox_q_gain[l], nh).reshape(1, fw)
        k_gain = jnp.tile(fox_k_gain[l], nh).reshape(1, fw)
        o_gain = jnp.tile(hg_o_gain[l], hg_heads).reshape(1, hvw)
        mix_gain = mix_norm[l].reshape(1, d)
        w1_in, w1_out = ffn1_w_in[l].astype(BF16), ffn1_w_out[l].astype(BF16)
        w2_in, w2_out = ffn2_w_in[l].astype(BF16), ffn2_w_out[l].astype(BF16)
        w_pf, w_ph, w_o = w_proj_fox[l].astype(BF16), w_proj_hg[l].astype(BF16), w_out[l].astype(BF16)

        yp = _ffn(yp, ffn1_norm[l].reshape(1, d), w1_in, w1_out)
        ys = _ffn(ys, ffn1_norm[l].reshape(1, d), w1_in, w1_out)

        w_vt = jnp.transpose(w_l[:, 2 * fw:3 * fw]).astype(BF16)
        inproj = functools.partial(_inproj, gain=mix_gain, w_mix=w_mix, w_vt=w_vt, gsum=gsum_fox, q_gain=q_gain,
                                   k_gain=k_gain, f_bias=f_bias, lb_logits=lb_logits,
                                   fw=fw, hw=hw, hvw=hvw, dh=dh, layer=l)
        merge = functools.partial(_merge, gain=mix_gain, w_gate=w_gate, o_gain=o_gain, gsum=gsum_hg,
                                  w_pf=w_pf, w_ph=w_ph, w_out=w_o, vd=vd)

        q, kf, vf, kb, v_t, lf, hq, hk, hlf, hv, hgate = inproj(yp, seqs=(bp, lp))
        c, kaug = _cumsum(lf.reshape(bp, lp, V7X_LANES), kb, tril_seq, nh=nh, dh=dh)
        c_row = jnp.transpose(c[:, :, :nh], (0, 2, 1))
        o_fox = _attn_prompt(q, kaug, v_t, c_row, b=bp, l=lp, dh=dh)
        s0_t = jnp.zeros((bp, hg_heads, vd, kd), F32)
        o_hg, s_t = _gla(hq, hk, hlf, hv, s0_t, b=bp, l=lp)
        yp = merge(yp, o_fox=o_fox, o_hg=o_hg, hgate=hgate)
        outs["kp"].append(kf.reshape(bp, nh, dh, lp))
        outs["vp"].append(vf.reshape(bp, nh, dh, lp))
        outs["fp"].append(lf[:, :nh].reshape(bp, lp, nh).astype(cache_logf.dtype))
        outs["sp"].append(jnp.swapaxes(s_t, -1, -2).astype(x_prompt.dtype))

        q, kf, vf, kb, vb, lf, hq, hk, hlf, hv, hgate = inproj(ys)
        o_fox = _attn_paged(cache_kt, cache_vt, cache_lft, page_table, q, kb, vb, lf, layer=l, t=ls, dh=dh)
        s0_t = jnp.swapaxes(state_hgrn[l].astype(F32), -1, -2)
        o_hg, s_t = _gla(hq, hk, hlf, hv, s0_t, b=bs, l=ls)
        ys = merge(ys, o_fox=o_fox, o_hg=o_hg, hgate=hgate)
        outs["ks"].append(kf.reshape(bs, ls, nh, dh))
        outs["vs"].append(vf.reshape(bs, ls, nh, dh))
        outs["fs"].append(lf[:, :nh].reshape(bs, ls, nh).astype(cache_logf.dtype))
        outs["ss"].append(jnp.swapaxes(s_t, -1, -2).astype(state_hgrn.dtype))

        yp = _ffn(yp, ffn2_norm[l].reshape(1, d), w2_in, w2_out)
        ys = _ffn(ys, ffn2_norm[l].reshape(1, d), w2_in, w2_out)

    st = lambda name: jnp.stack(outs[name])
    seq_last = lambda name: jnp.transpose(st(name), (0, 1, 4, 2, 3))
    return (yp.reshape(bp, lp, d), ys.reshape(bs, ls, d), seq_last("kp"), seq_last("vp"), st("fp"), st("sp"),
            st("ks"), st("vs"), st("fs"), st("ss"))
```

```python
import functools

import jax
import jax.numpy as jnp
import numpy as np
from jax import lax
from jax.experimental import pallas as pl
from jax.experimental.pallas import tpu as pltpu

F32 = jnp.float32
BF16 = jnp.bfloat16
EPS = 1e-6
NEG_INF = float("-inf")
LOG2E = 1.4426950408889634

V7X_LANES = 128
V7X_SUBLANES = 8
V7X_VMEM_BYTES = 64 * 1024 * 1024
VMEM_LIMIT = (V7X_VMEM_BYTES * 3) // 4

ROW_TILE = 512
ATTN_TILE = 512
ATTN_UNROLL = 3
GLA_CHUNK = 64
GLA_LEAF = V7X_SUBLANES
GLA_ROWS = 512
PAGES_PER_STEP = 32


def _params(*sem):
    return pltpu.CompilerParams(dimension_semantics=sem, vmem_limit_bytes=VMEM_LIMIT)


def _resident(shape):
    nd = len(shape)
    return pl.BlockSpec(shape, lambda *_: (0,) * nd, pipeline_mode=pl.Buffered(1))


def _dot(a, b):
    return jnp.dot(a, b, preferred_element_type=F32)


def _dot_nt(a, b):
    return lax.dot_general(a, b, (((1,), (1,)), ((), ())), preferred_element_type=F32)


def _dot_tn(a, b):
    return lax.dot_general(a, b, (((0,), (0,)), ((), ())), preferred_element_type=F32)


def _split3(x):
    hi = x.astype(BF16)
    r1 = x - hi.astype(F32)
    mid = r1.astype(BF16)
    lo = (r1 - mid.astype(F32)).astype(BF16)
    return hi, mid, lo


def _dot01(m01, x):
    hi, mid, lo = _split3(x)
    return _dot(m01, hi) + _dot(m01, mid) + _dot(m01, lo)


def _dot01_nt(m01, x):
    hi, mid, lo = _split3(x)
    return _dot_nt(m01, hi) + _dot_nt(m01, mid) + _dot_nt(m01, lo)


def _dot01_r(x, m01):
    hi, mid, lo = _split3(x)
    return _dot(hi, m01) + _dot(mid, m01) + _dot(lo, m01)


def _rms_rows(x, gain):
    return x * lax.rsqrt(jnp.mean(x * x, axis=-1, keepdims=True) + EPS) * gain


def _sigmoid(x):
    return 1.0 / (1.0 + jnp.exp(-x))


def _ffn_body(x_ref, g_ref, wi_ref, wo_ref, o_ref, acc_ref, *, d_ff, tf):
    x = x_ref[...]
    h = _rms_rows(x, g_ref[...]).astype(BF16)
    for j in range(d_ff // tf):
        a = _dot(h, wi_ref[:, j * tf:(j + 1) * tf])
        b = _dot(h, wi_ref[:, d_ff + j * tf:d_ff + (j + 1) * tf])
        g = (a * _sigmoid(a) * b).astype(BF16)
        y = _dot(g, wo_ref[j * tf:(j + 1) * tf, :])
        if j == 0:
            acc_ref[...] = y
        else:
            acc_ref[...] += y
    o_ref[...] = x + 0.5 * acc_ref[...]


def _ffn(x, gain, w_in, w_out):
    n, d = x.shape
    d_ff = w_out.shape[0]
    tm = min(ROW_TILE, n)
    tf = 2 * V7X_LANES if d_ff % (2 * V7X_LANES) == 0 else V7X_LANES
    assert n % tm == 0 and d_ff % tf == 0
    return pl.pallas_call(
        functools.partial(_ffn_body, d_ff=d_ff, tf=tf),
        grid=(n // tm,),
        in_specs=[pl.BlockSpec((tm, d), lambda i: (i, 0)),
                  _resident((1, d)),
                  _resident((d, 2 * d_ff)),
                  _resident((d_ff, d))],
        out_specs=pl.BlockSpec((tm, d), lambda i: (i, 0)),
        out_shape=jax.ShapeDtypeStruct((n, d), F32),
        scratch_shapes=[pltpu.VMEM((tm, d), F32)],
        compiler_params=_params("parallel"),
        name="ffn",
    )(x, gain, w_in, w_out)


def _inproj_body(x_ref, g_ref, w_ref, wvt_ref, gsum_ref, qg_ref, kg_ref, fb_ref, lbl_ref,
                 q_ref, kf_ref, vf_ref, kb_ref, vb_ref, lf_ref,
                 hq_ref, hk_ref, hlf_ref, hv_ref, hg_ref, *, fw, hw, hvw, dh, layer, seq_major):
    h = _rms_rows(x_ref[...], g_ref[...]).astype(BF16)

    def proj(c0, n):
        return _dot(h, w_ref[:, c0:c0 + n])

    gsum = gsum_ref[...]

    def head_norm(z, gain):
        ms = _dot((z * z).astype(BF16), gsum) * (1.0 / dh)
        return z * lax.rsqrt(ms + EPS) * gain

    q = head_norm(proj(0, fw), qg_ref[...])
    q_ref[...] = (q * (dh ** -0.5 * LOG2E)).astype(BF16)
    k = head_norm(proj(fw, fw), kg_ref[...])
    kb_ref[...] = k.astype(BF16)
    if seq_major:
        v_t = _dot_nt(wvt_ref[...], h)
        kf_ref[...] = k.T
        vf_ref[...] = v_t
        vb_ref[...] = v_t.astype(BF16)
    else:
        v = proj(2 * fw, fw)
        kf_ref[...] = k
        vf_ref[...] = v
        vb_ref[...] = v.astype(BF16)

    c0 = 3 * fw
    hq = proj(c0, hw)
    hq_ref[...] = hq * _sigmoid(hq)

    lg = lbl_ref[...]
    e = jnp.exp(lg - jnp.max(lg, axis=0, keepdims=True))
    sm = e / jnp.sum(e, axis=0, keepdims=True)
    lb = jnp.zeros_like(sm[0:1])
    for i in range(1, layer + 1):
        lb = lb + sm[i:i + 1]
    log_lb = jnp.log(lb)
    log_1m_lb = jnp.log1p(-lb)

    z = proj(c0 + hw, hw)
    ez = jnp.exp(-jnp.abs(z))
    log_sig = jnp.minimum(z, 0.0) - jnp.log1p(ez)
    t = log_1m_lb + log_sig
    mx = jnp.maximum(log_lb, t)
    hlf_ref[...] = mx + jnp.log1p(jnp.exp(-jnp.abs(log_lb - t)))
    hk_ref[...] = (1.0 - lb) * (jnp.where(z >= 0.0, ez, 1.0) / (1.0 + ez))

    hv_ref[...] = proj(c0 + 2 * hw, hvw).astype(BF16)
    hg_ref[...] = proj(c0 + 2 * hw + hvw, hvw)

    zf = proj(c0 + 2 * hw + 2 * hvw, V7X_LANES) + fb_ref[...]
    lf_ref[...] = jnp.minimum(zf, 0.0) - jnp.log1p(jnp.exp(-jnp.abs(zf)))


def _inproj(x, gain, w_mix, w_vt, gsum, q_gain, k_gain, f_bias, lb_logits, *, fw, hw, hvw, dh, layer, seqs=None):
    n, d = x.shape
    tm = min(ROW_TILE, n)
    assert n % tm == 0
    row = lambda w: pl.BlockSpec((tm, w), lambda i: (i, 0))
    sds = lambda w, dt: jax.ShapeDtypeStruct((n, w), dt)
    if seqs is None:
        kv_specs = [row(fw), row(fw), row(fw)]
        kv_shapes = [sds(fw, F32), sds(fw, F32), sds(fw, BF16)]
    else:
        b, l = seqs
        nt = l // tm
        assert b * l == n and l % tm == 0
        kv_specs = [pl.BlockSpec((None, fw, tm), lambda i: (i // nt, 0, i % nt))] * 3
        kv_shapes = [jax.ShapeDtypeStruct((b, fw, l), dt) for dt in (F32, F32, BF16)]
    return pl.pallas_call(
        functools.partial(_inproj_body, fw=fw, hw=hw, hvw=hvw, dh=dh, layer=layer, seq_major=seqs is not None),
        grid=(n // tm,),
        in_specs=[row(d), _resident((1, d)), _resident(w_mix.shape), _resident(w_vt.shape), _resident(gsum.shape),
                  _resident((1, fw)), _resident((1, fw)), _resident((1, V7X_LANES)),
                  _resident(lb_logits.shape)],
        out_specs=[row(fw), kv_specs[0], kv_specs[1], row(fw), kv_specs[2], row(V7X_LANES),
                   row(hw), row(hw), row(hw), row(hvw), row(hvw)],
        out_shape=[sds(fw, BF16), kv_shapes[0], kv_shapes[1], sds(fw, BF16), kv_shapes[2],
                   sds(V7X_LANES, F32),
                   sds(hw, F32), sds(hw, F32), sds(hw, F32), sds(hvw, BF16), sds(hvw, F32)],
        compiler_params=_params("parallel"),
        name="inproj",
    )(x, gain, w_mix, w_vt, gsum, q_gain, k_gain, f_bias, lb_logits)


BIAS_PIECES = 3


def _bias_lane(hh, dh):
    return ((hh + 1) % (V7X_LANES // dh)) * dh


def _cumsum_body(lf_ref, k_ref, tril_ref, place_ref, c_ref, kaug_ref, carry_ref, *, nh, dh):
    @pl.when(pl.program_id(1) == 0)
    def _():
        carry_ref[...] = jnp.zeros_like(carry_ref)

    c = _dot01(tril_ref[...], lf_ref[...]) + carry_ref[...]
    c_ref[...] = c
    carry_ref[...] = c[-1:, :]
    pieces = jnp.concatenate(_split3(-LOG2E * c), axis=1)
    lane = lax.broadcasted_iota(jnp.int32, (1, V7X_LANES), 1)
    hp = V7X_LANES // dh
    placed = _dot(pieces, place_ref[...]).astype(BF16)
    for h in range(nh):
        tile = h // hp
        keys = k_ref[:, tile * V7X_LANES:(tile + 1) * V7X_LANES]
        kaug_ref[h] = jnp.where((lane // dh) == (h % hp), keys, placed[:, h * V7X_LANES:(h + 1) * V7X_LANES])


def _cumsum(lf, k, tril, *, nh, dh):
    b, l, w = lf.shape
    tc = tril.shape[0]
    nt = l // tc
    place = np.zeros((BIAS_PIECES * w, nh * V7X_LANES), np.float32)
    for h in range(nh):
        for piece in range(BIAS_PIECES):
            place[piece * w + h, h * V7X_LANES + _bias_lane(h % (V7X_LANES // dh), dh) + piece] = 1.0
    place = jnp.asarray(place, BF16)
    return pl.pallas_call(
        functools.partial(_cumsum_body, nh=nh, dh=dh),
        grid=(b, nt),
        in_specs=[pl.BlockSpec((None, tc, w), lambda i, j: (i, j, 0)),
                  pl.BlockSpec((tc, k.shape[1]), lambda i, j: (i * nt + j, 0)),
                  _resident(tril.shape), _resident(place.shape)],
        out_specs=[pl.BlockSpec((None, tc, w), lambda i, j: (i, j, 0)),
                   pl.BlockSpec((None, nh, tc, V7X_LANES), lambda i, j: (i, 0, j, 0))],
        out_shape=[jax.ShapeDtypeStruct((b, l, w), F32), jax.ShapeDtypeStruct((b, nh, l, V7X_LANES), BF16)],
        scratch_shapes=[pltpu.VMEM((1, w), F32)],
        compiler_params=_params("parallel", "arbitrary"),
        name="logf_cumsum",
    )(lf, k, tril, place)


def _attn_body(q_ref, kaug_ref, vt_ref, crow_ref, o_ref, m_ref, acc_ref, u_ref, *, t, dh):
    p = pl.program_id(1)
    qi = pl.program_id(2)
    hp = V7X_LANES // dh
    q = q_ref[...]
    lane = lax.broadcasted_iota(jnp.int32, (1, V7X_LANES), 1)
    key_i = lax.broadcasted_iota(jnp.int32, (t, t), 0)
    qry_i = lax.broadcasted_iota(jnp.int32, (t, t), 1)
    feat = lax.broadcasted_iota(jnp.int32, (V7X_LANES, t), 0)
    qas, cts, keep, ones = [], [], [], []
    for hh in range(hp):
        b0 = _bias_lane(hh, dh)
        one_lanes = jnp.where((lane >= b0) & (lane < b0 + BIAS_PIECES), 1.0, 0.0).astype(BF16)
        qas.append(jnp.where((lane // dh) == hh, q, one_lanes))
        cts.append(LOG2E * crow_ref[pl.ds(p * hp + hh, 1), :])
        keep.append(jnp.where((feat // dh) == hh, 1.0, 0.0).astype(BF16))
        ones.append(jnp.where(feat == b0, 1.0, 0.0).astype(BF16))
        m_ref[hh] = jnp.full(m_ref.shape[1:], NEG_INF, F32)
        acc_ref[hh] = jnp.zeros(acc_ref.shape[1:], F32)

    heads = range(hp)

    def scores(kb):
        r0 = pl.multiple_of(kb * t, t)
        return [_dot_nt(kaug_ref[hh, pl.ds(r0, t), :], qas[hh]) for hh in heads]

    def prefetch_scores(kb, slot):
        for hh, u in enumerate(scores(kb)):
            u_ref[slot, hh] = u

    def steps(group, kbs, diagonal_last):
        slot = group % 2
        r0s = [pl.multiple_of(kb * t, t) for kb in kbs]
        us = [[u_ref[slot, hh] for hh in heads]] + [scores(kb) for kb in kbs[1:]]
        if not diagonal_last:
            prefetch_scores(kbs[-1] + 1, 1 - slot)
        for n, (r0, u_heads) in enumerate(zip(r0s, us)):
            vt = vt_ref[:, pl.ds(r0, t)]
            if diagonal_last and n == len(kbs) - 1:
                u_heads = [jnp.where(key_i <= qry_i, u, NEG_INF) for u in u_heads]
            m_old = [m_ref[hh] for hh in heads]
            m_new = [jnp.maximum(m_old[hh], jnp.max(u_heads[hh], axis=0, keepdims=True) + cts[hh]) for hh in heads]
            pe = [jnp.exp2(u_heads[hh] - (m_new[hh] - cts[hh])).astype(BF16) for hh in heads]
            pv = [_dot(vt * keep[hh] + ones[hh], pe[hh]) for hh in heads]
            for hh in heads:
                acc_ref[hh] = jnp.exp2(m_old[hh] - m_new[hh]) * acc_ref[hh] + pv[hh]
                m_ref[hh] = m_new[hh]

    def body(i, carry):
        steps(i, [ATTN_UNROLL * i + r for r in range(ATTN_UNROLL)], False)
        return carry

    prefetch_scores(0, 0)
    lax.fori_loop(0, qi // ATTN_UNROLL, body, 0)
    for rem in range(ATTN_UNROLL):
        @pl.when(qi % ATTN_UNROLL == rem)
        def _(rem=rem):
            steps(qi // ATTN_UNROLL, [qi - rem + r for r in range(rem)] + [qi], True)
    o_t = None
    for hh in range(hp):
        b0 = _bias_lane(hh, dh)
        acc = acc_ref[hh]
        o_h = acc / acc[b0:b0 + 1, :]
        o_t = o_h if o_t is None else jnp.where((feat // dh) == hh, o_h, o_t)
    o_ref[...] = o_t.T.astype(o_ref.dtype)


def _attn_prompt(q, kaug, vt, crow, *, b, l, dh):
    n, fw = q.shape
    t = min(ATTN_TILE, l)
    nq = l // t
    hp = V7X_LANES // dh
    assert l % t == 0 and fw % V7X_LANES == 0 and V7X_LANES % dh == 0 and BIAS_PIECES < dh
    return pl.pallas_call(
        functools.partial(_attn_body, t=t, dh=dh),
        grid=(b, fw // V7X_LANES, nq),
        in_specs=[pl.BlockSpec((t, V7X_LANES), lambda i, p, j: (i * nq + j, p)),
                  pl.BlockSpec((None, hp, l, V7X_LANES), lambda i, p, j: (i, p, 0, 0)),
                  pl.BlockSpec((None, V7X_LANES, l), lambda i, p, j: (i, p, 0)),
                  pl.BlockSpec((None, crow.shape[1], t), lambda i, p, j: (i, 0, j))],
        out_specs=pl.BlockSpec((t, V7X_LANES), lambda i, p, j: (i * nq + j, p)),
        out_shape=jax.ShapeDtypeStruct((n, fw), BF16),
        scratch_shapes=[pltpu.VMEM((hp, 1, t), F32), pltpu.VMEM((hp, V7X_LANES, t), F32),
                        pltpu.VMEM((2, hp, t, t), F32)],
        compiler_params=_params("parallel", "parallel", "arbitrary"),
        name="fox_prompt",
    )(q, kaug, vt, crow)


def _gla_masks(c):
    r = np.arange(c)[:, None]
    s = np.arange(c)[None, :]
    tril = (s <= r).astype(np.float32)
    levels = []
    m = c // 2
    while m >= GLA_LEAF:
        levels.append((((r ^ s) < 2 * m) & ((r & m) != 0) & ((s & m) == 0)).astype(np.float32))
        m //= 2
    if not levels:
        levels.append(np.zeros((c, c), np.float32))
    return jnp.asarray(tril, BF16), jnp.asarray(np.stack(levels), F32)


def _gla_chunk(q, k, lf, v, states, tril, lvl_ref, leaf_refs, *, c, nh, kd, vd):
    def head(x, h, w):
        return x[:, h * w:(h + 1) * w]

    b = _dot01(tril, LOG2E * lf)
    q = q * (kd ** -0.5)
    b_last = b[c - 1:c, :]
    q_hat = (q * jnp.exp2(b)).astype(BF16)
    k_hat = (k * jnp.exp2(b_last - b)).astype(BF16)
    decay_last = jnp.exp2(b_last)
    outs = [_dot_nt(head(q_hat, h, kd), states[h].astype(BF16)) for h in range(nh)]
    new_states = [states[h] * head(decay_last, h, kd) + _dot_tn(head(v, h, vd), head(k_hat, h, kd))
                  for h in range(nh)]

    m = c // 2
    lvl = 0
    scores = [None] * nh
    while m >= GLA_LEAF:
        pieces = []
        for blk in range(c // (2 * m)):
            mid = blk * 2 * m + m
            pieces.append(jnp.broadcast_to(b[mid - 1:mid, :], (2 * m, b.shape[1])))
        ref = pieces[0] if len(pieces) == 1 else jnp.concatenate(pieces, axis=0)
        e = jnp.exp2(-jnp.abs(b - ref))
        qe = (q * e).astype(BF16)
        ke = (k * e).astype(BF16)
        mask = lvl_ref[lvl]
        for h in range(nh):
            part = mask * _dot_nt(head(qe, h, kd), head(ke, h, kd))
            scores[h] = part if scores[h] is None else scores[h] + part
        m //= 2
        lvl += 1
    if scores[0] is not None:
        outs = [outs[h] + _dot(scores[h].astype(BF16), head(v, h, vd)) for h in range(nh)]

    nl = c // GLA_LEAF
    b_ref, q_ref, k_ref, v_ref, o_ref = leaf_refs
    v32 = v.astype(F32)
    for h in range(nh):
        b_ref[h] = head(b, h, kd)
        q_ref[h] = head(q, h, kd)
        k_ref[h] = head(k, h, kd)
        v_ref[h] = head(v32, h, vd)

    def rows(ref, i):
        return jnp.concatenate([ref[h, pl.ds(i, nl, stride=GLA_LEAF), :] for h in range(nh)], axis=1)

    bs = [rows(b_ref, i) for i in range(GLA_LEAF)]
    ks = [rows(k_ref, i) for i in range(GLA_LEAF)]
    vs = [rows(v_ref, i) for i in range(GLA_LEAF)]
    for t in range(GLA_LEAF):
        q_t = rows(q_ref, t)
        o_t = [None] * nh
        for s in range(t + 1):
            prod = jnp.exp2(bs[t] - bs[s]) * q_t * ks[s]
            for h in range(nh):
                w = jnp.sum(head(prod, h, kd), axis=-1, keepdims=True)
                term = w * head(vs[s], h, vd)
                o_t[h] = term if o_t[h] is None else o_t[h] + term
        for h in range(nh):
            o_ref[h, pl.ds(t, nl, stride=GLA_LEAF), :] = o_t[h]
    return jnp.concatenate([outs[h] + o_ref[h] for h in range(nh)], axis=1), new_states


def _gla_body(q_ref, k_ref, lf_ref, v_ref, s0_ref, tril_ref, lvl_ref, o_ref, sT_ref, st_ref, *leaf_refs,
              c, rows, nh, kd, vd):
    j = pl.program_id(1)

    @pl.when(j == 0)
    def _():
        st_ref[...] = s0_ref[...]

    tril = tril_ref[...]

    def chunk(ci, carry):
        rs = pl.ds(pl.multiple_of(ci * c, c), c)
        o, new_states = _gla_chunk(q_ref[rs, :], k_ref[rs, :], lf_ref[rs, :], v_ref[rs, :],
                                   [st_ref[h] for h in range(nh)], tril, lvl_ref, leaf_refs,
                                   c=c, nh=nh, kd=kd, vd=vd)
        o_ref[rs, :] = o
        for h in range(nh):
            st_ref[h] = new_states[h]
        return carry

    if rows == c:
        chunk(0, 0)
    else:
        lax.fori_loop(0, rows // c, chunk, 0, unroll=4)

    @pl.when(j == pl.num_programs(1) - 1)
    def _():
        sT_ref[...] = st_ref[...]


def _gla(q, k, lf, v, s0_t, *, b, l):
    n, hw = q.shape
    hvw = v.shape[1]
    nh, vd, kd = s0_t.shape[1:]
    c = min(GLA_CHUNK, l)
    rows = min(GLA_ROWS, l)
    assert l % rows == 0 and rows % c == 0 and c % GLA_LEAF == 0
    tril, lvls = _gla_masks(c)
    nr = l // rows
    row = lambda w: pl.BlockSpec((rows, w), lambda i, j: (i * nr + j, 0))
    state = pl.BlockSpec((None, nh, vd, kd), lambda i, j: (i, 0, 0, 0))
    return pl.pallas_call(
        functools.partial(_gla_body, c=c, rows=rows, nh=nh, kd=kd, vd=vd),
        grid=(b, nr),
        in_specs=[row(hw), row(hw), row(hw), row(hvw), state, _resident(tril.shape), _resident(lvls.shape)],
        out_specs=[row(hvw), state],
        out_shape=[jax.ShapeDtypeStruct((n, hvw), F32), jax.ShapeDtypeStruct(s0_t.shape, F32)],
        scratch_shapes=[pltpu.VMEM((nh, vd, kd), F32)]
        + [pltpu.VMEM((nh, c, w), F32) for w in (kd, kd, kd, vd, vd)],
        compiler_params=_params("parallel", "arbitrary"),
        name="hgrn2",
    )(q, k, lf, v, s0_t, tril, lvls)


def _merge_body(x_ref, g_ref, wg_ref, of_ref, oh_ref, hg_ref, og_ref, gsum_ref, wpf_ref, wph_ref, wo_ref,
                o_ref, *, d, vd):
    x = x_ref[...]
    h = _rms_rows(x, g_ref[...]).astype(BF16)
    gate_fox = _sigmoid(_dot(h, wg_ref[:, :d]))
    gate_hg = _sigmoid(_dot(h, wg_ref[:, d:]))
    y_fox = _dot(of_ref[...], wpf_ref[...])
    oh = oh_ref[...]
    ms = _dot((oh * oh).astype(BF16), gsum_ref[...]) * (1.0 / vd)
    hg = hg_ref[...]
    ob = (oh * lax.rsqrt(ms + EPS) * og_ref[...]) * (hg * _sigmoid(hg))
    y_hg = _dot(ob.astype(BF16), wph_ref[...])
    y = gate_fox * y_fox + gate_hg * y_hg
    o_ref[...] = x + _dot(y.astype(BF16), wo_ref[...])


def _merge(x, gain, w_gate, o_fox, o_hg, hgate, o_gain, gsum, w_pf, w_ph, w_out, *, vd):
    n, d = x.shape
    tm = min(ROW_TILE, n)
    assert n % tm == 0
    row = lambda w: pl.BlockSpec((tm, w), lambda i: (i, 0))
    return pl.pallas_call(
        functools.partial(_merge_body, d=d, vd=vd),
        grid=(n // tm,),
        in_specs=[row(d), _resident((1, d)), _resident(w_gate.shape), row(o_fox.shape[1]), row(o_hg.shape[1]),
                  row(hgate.shape[1]), _resident(o_gain.shape), _resident(gsum.shape),
                  _resident(w_pf.shape), _resident(w_ph.shape), _resident(w_out.shape)],
        out_specs=row(d),
        out_shape=jax.ShapeDtypeStruct((n, d), F32),
        compiler_params=_params("parallel"),
        name="merge",
    )(x, gain, w_gate, o_fox, o_hg, hgate, o_gain, gsum, w_pf, w_ph, w_out)


def _paged_body(pt_ref, *refs, g, ps, nh, dh, t):
    k_refs = refs[:g]
    v_refs = refs[g:2 * g]
    lf_refs = refs[2 * g:3 * g]
    (w_ref, q_ref, kn_ref, vn_ref, lfn_ref, o_ref,
     qbd_ref, ctcol_ref, carry_ref, m_ref, l_ref, acc_ref) = refs[3 * g:]
    j = pl.program_id(1)
    rows = nh * t
    fw = nh * dh
    row_head = lax.broadcasted_iota(jnp.int32, (rows, 1), 0) // t

    def new_token_cumsum():
        lfn = lfn_ref[...]
        sub = lax.broadcasted_iota(jnp.int32, lfn.shape, 0)
        ct = jnp.zeros_like(lfn)
        for s in range(t):
            ct = ct + jnp.where(sub >= s, lfn[s:s + 1, :], 0.0)
        return ct

    @pl.when(j == 0)
    def _():
        lane_head = lax.broadcasted_iota(jnp.int32, (rows, fw), 1) // dh
        q_rep = jnp.concatenate([q_ref[...]] * nh, axis=0)
        qbd_ref[...] = jnp.where(lane_head == row_head, q_rep, jnp.zeros_like(q_rep))
        ct_rep = jnp.concatenate([new_token_cumsum()] * nh, axis=0)
        lane = lax.broadcasted_iota(jnp.int32, ct_rep.shape, 1)
        ctcol_ref[...] = LOG2E * jnp.sum(jnp.where(lane == row_head, ct_rep, 0.0), axis=-1, keepdims=True)
        carry_ref[...] = jnp.zeros_like(carry_ref)
        m_ref[...] = jnp.full_like(m_ref, NEG_INF)
        l_ref[...] = jnp.zeros_like(l_ref)
        acc_ref[...] = jnp.zeros_like(acc_ref)

    qbd = qbd_ref[...]
    ctcol = ctcol_ref[...]

    def update(u, pv_fn):
        m_old = m_ref[...]
        m_new = jnp.maximum(m_old, jnp.max(u, axis=-1, keepdims=True) + ctcol)
        pe = jnp.exp2(u - (m_new - ctcol))
        alpha = jnp.exp2(m_old - m_new)
        l_ref[...] = alpha * l_ref[...] + jnp.sum(pe, axis=-1, keepdims=True)
        acc_ref[...] = alpha * acc_ref[...] + pv_fn(pe)
        m_ref[...] = m_new

    lf_all = jnp.concatenate([lf_refs[i][...] for i in range(g)], axis=0)
    sums = _dot01_r(lf_all, w_ref[...])
    carry = carry_ref[...]
    biases = [None] * g
    for i in reversed(range(g)):
        biases[i] = sums[i * nh:(i + 1) * nh, :ps] + carry
        carry = carry + sums[i * nh:(i + 1) * nh, ps:]
    carry_ref[...] = carry
    bias = LOG2E * jnp.concatenate(biases, axis=1)
    bias = jnp.broadcast_to(bias[:, None, :], (nh, t, g * ps)).reshape(rows, g * ps)

    s = jnp.concatenate([_dot(qbd, k_refs[i][...].astype(BF16)) for i in range(g)], axis=1)

    def pv_past(pe):
        out = None
        for i in range(g):
            part = _dot_nt(pe[:, i * ps:(i + 1) * ps].astype(BF16), v_refs[i][...].astype(BF16))
            out = part if out is None else out + part
        return out

    update(s + bias, pv_past)

    @pl.when(j == pl.num_programs(1) - 1)
    def _():
        pad = jnp.zeros((ps - t, fw), BF16)
        kn = jnp.concatenate([kn_ref[...], pad], axis=0)
        vn = jnp.concatenate([vn_ref[...], pad], axis=0)
        ct = new_token_cumsum()
        ct_pad = jnp.concatenate([ct, jnp.zeros((ps - t, ct.shape[1]), F32)], axis=0)
        lane = lax.broadcasted_iota(jnp.int32, (rows, ct.shape[1]), 1)
        onehot = jnp.where(lane == row_head, 1.0, 0.0).astype(BF16)
        c_s = _dot01_nt(onehot, ct_pad)
        col = lax.broadcasted_iota(jnp.int32, (rows, ps), 1)
        row_t = lax.broadcasted_iota(jnp.int32, (rows, ps), 0) % t
        u = jnp.where(col <= row_t, _dot_nt(qbd, kn) - LOG2E * c_s, NEG_INF)
        update(u, lambda pe: _dot(pe.astype(BF16), vn))
        o = acc_ref[...] / l_ref[...]
        lane_head = lax.broadcasted_iota(jnp.int32, (rows, fw), 1) // dh
        o = jnp.where(lane_head == row_head, o, 0.0)
        o_ref[...] = jnp.sum(o.reshape(nh, t, fw), axis=0).astype(o_ref.dtype)


def _attn_paged(cache_kt, cache_vt, cache_lft, page_table, q, k_new, v_new, lf_new, *, layer, t, dh):
    _, _, fw, ps = cache_kt.shape
    nb, n_pages = page_table.shape
    nh = fw // dh
    g = min(PAGES_PER_STEP, n_pages)
    ng = n_pages // g
    rows = nh * t
    assert n_pages % g == 0 and t <= ps and cache_lft.shape[2:] == (nh, ps)
    j_idx = np.arange(ps)
    w = np.concatenate([(j_idx[:, None] > j_idx[None, :]), np.ones((ps, ps), bool)], axis=1)
    w = jnp.asarray(w.astype(np.float32), BF16)

    def page(i, width):
        return pl.BlockSpec((None, None, width, ps),
                            lambda b, j, pt: (layer, pt[b, (ng - 1 - j) * g + i], 0, 0))

    new_rows = lambda width: pl.BlockSpec((t, width), lambda b, j, pt: (b, 0))
    grid_spec = pltpu.PrefetchScalarGridSpec(
        num_scalar_prefetch=1,
        grid=(nb, ng),
        in_specs=[page(i, fw) for i in range(g)] + [page(i, fw) for i in range(g)]
        + [page(i, nh) for i in range(g)]
        + [pl.BlockSpec(w.shape, lambda b, j, pt: (0, 0)),
           new_rows(fw), new_rows(fw), new_rows(fw), new_rows(lf_new.shape[1])],
        out_specs=new_rows(fw),
        scratch_shapes=[pltpu.VMEM((rows, fw), BF16), pltpu.VMEM((rows, 1), F32), pltpu.VMEM((nh, ps), F32),
                        pltpu.VMEM((rows, 1), F32), pltpu.VMEM((rows, 1), F32), pltpu.VMEM((rows, fw), F32)],
    )
    return pl.pallas_call(
        functools.partial(_paged_body, g=g, ps=ps, nh=nh, dh=dh, t=t),
        grid_spec=grid_spec,
        out_shape=jax.ShapeDtypeStruct((nb * t, fw), BF16),
        compiler_params=_params("parallel", "arbitrary"),
        name="fox_paged",
    )(page_table, *([cache_kt] * g), *([cache_vt] * g), *([cache_lft] * g), w, q, k_new, v_new, lf_new)


def _block_ones(width, group):
    i = np.arange(width) // group
    return jnp.asarray((i[:, None] == i[None, :]).astype(np.float32), BF16)


def kernel(x_prompt, x_sample, cache_k, cache_v, cache_logf, state_hgrn, page_table, ffn1_norm, ffn1_w_in, ffn1_w_out, mix_norm, w_in, fox_f_bias, fox_q_gain, fox_k_gain, hg_lb_logits, hg_o_gain, w_proj_fox, w_proj_hg, w_out, ffn2_norm, ffn2_w_in, ffn2_w_out):
    depth = w_in.shape[0]
    bp, lp, d = x_prompt.shape
    bs, ls, _ = x_sample.shape
    nh, dh = cache_k.shape[-2:]
    fw = nh * dh
    hg_heads, kd, vd = state_hgrn.shape[-3:]
    hw, hvw = hg_heads * kd, hg_heads * vd
    assert nh <= V7X_LANES and w_in.shape[-1] == 3 * fw + nh + 2 * hw + 2 * hvw + 2 * d

    gsum_fox = _block_ones(fw, dh)
    gsum_hg = _block_ones(hvw, vd)
    tc = min(ROW_TILE, lp)
    tril_seq = jnp.asarray(np.tril(np.ones((tc, tc), np.float32)), BF16)
    lb_logits = hg_lb_logits.astype(F32)
    cache_kt = jnp.transpose(cache_k, (0, 1, 3, 4, 2)).reshape(cache_k.shape[:2] + (fw, cache_k.shape[2]))
    cache_vt = jnp.transpose(cache_v, (0, 1, 3, 4, 2)).reshape(cache_v.shape[:2] + (fw, cache_v.shape[2]))
    cache_lft = jnp.transpose(cache_logf, (0, 1, 3, 2)).astype(F32)

    yp = x_prompt.reshape(bp * lp, d)
    ys = x_sample.reshape(bs * ls, d)
    outs = {name: [] for name in ("kp", "vp", "fp", "sp", "ks", "vs", "fs", "ss")}

    for l in range(depth):
        c_mix = 3 * fw + nh
        w_l = w_in[l]
        w_mix = jnp.concatenate(
            [w_l[:, :3 * fw], w_l[:, c_mix:c_mix + 2 * hw + 2 * hvw],
             jnp.pad(w_l[:, 3 * fw:c_mix], ((0, 0), (0, V7X_LANES - nh)))], axis=1).astype(BF16)
        w_gate = w_l[:, c_mix + 2 * hw + 2 * hvw:].astype(BF16)
        f_bias = jnp.pad(fox_f_bias[l], (0, V7X_LANES - nh)).reshape(1, V7X_LANES)
        q_gain = jnp.tile(fox_q_gain[l], nh).reshape(1, fw)
        k_gain = jnp.tile(fox_k_gain[l], nh).reshape(1, fw)
        o_gain = jnp.tile(hg_o_gain[l], hg_heads).reshape(1, hvw)
        mix_gain = mix_norm[l].reshape(1, d)
        w1_in, w1_out = ffn1_w_in[l].astype(BF16), ffn1_w_out[l].astype(BF16)
        w2_in, w2_out = ffn2_w_in[l].astype(BF16), ffn2_w_out[l].astype(BF16)
        w_pf, w_ph, w_o = w_proj_fox[l].astype(BF16), w_proj_hg[l].astype(BF16), w_out[l].astype(BF16)

        yp = _ffn(yp, ffn1_norm[l].reshape(1, d), w1_in, w1_out)
        ys = _ffn(ys, ffn1_norm[l].reshape(1, d), w1_in, w1_out)

        w_vt = jnp.transpose(w_l[:, 2 * fw:3 * fw]).astype(BF16)
        inproj = functools.partial(_inproj, gain=mix_gain, w_mix=w_mix, w_vt=w_vt, gsum=gsum_fox, q_gain=q_gain,
                                   k_gain=k_gain, f_bias=f_bias, lb_logits=lb_logits,
                                   fw=fw, hw=hw, hvw=hvw, dh=dh, layer=l)
        merge = functools.partial(_merge, gain=mix_gain, w_gate=w_gate, o_gain=o_gain, gsum=gsum_hg,
                                  w_pf=w_pf, w_ph=w_ph, w_out=w_o, vd=vd)

        q, kf, vf, kb, v_t, lf, hq, hk, hlf, hv, hgate = inproj(yp, seqs=(bp, lp))
        c, kaug = _cumsum(lf.reshape(bp, lp, V7X_LANES), kb, tril_seq, nh=nh, dh=dh)
        c_row = jnp.transpose(c[:, :, :nh], (0, 2, 1))
        o_fox = _attn_prompt(q, kaug, v_t, c_row, b=bp, l=lp, dh=dh)
        s0_t = jnp.zeros((bp, hg_heads, vd, kd), F32)
        o_hg, s_t = _gla(hq, hk, hlf, hv, s0_t, b=bp, l=lp)
        yp = merge(yp, o_fox=o_fox, o_hg=o_hg, hgate=hgate)
        outs["kp"].append(kf.reshape(bp, nh, dh, lp))
        outs["vp"].append(vf.reshape(bp, nh, dh, lp))
        outs["fp"].append(lf[:, :nh].reshape(bp, lp, nh).astype(cache_logf.dtype))
        outs["sp"].append(jnp.swapaxes(s_t, -1, -2).astype(x_prompt.dtype))

        q, kf, vf, kb, vb, lf, hq, hk, hlf, hv, hgate = inproj(ys)
        o_fox = _attn_paged(cache_kt, cache_vt, cache_lft, page_table, q, kb, vb, lf, layer=l, t=ls, dh=dh)
        s0_t = jnp.swapaxes(state_hgrn[l].astype(F32), -1, -2)
        o_hg, s_t = _gla(hq, hk, hlf, hv, s0_t, b=bs, l=ls)
        ys = merge(ys, o_fox=o_fox, o_hg=o_hg, hgate=hgate)
        outs["ks"].append(kf.reshape(bs, ls, nh, dh))
        outs["vs"].append(vf.reshape(bs, ls, nh, dh))
        outs["fs"].append(lf[:, :nh].reshape(bs, ls, nh).astype(cache_logf.dtype))
        outs["ss"].append(jnp.swapaxes(s_t, -1, -2).astype(state_hgrn.dtype))

        yp = _ffn(yp, ffn2_norm[l].reshape(1, d), w2_in, w2_out)
        ys = _ffn(ys, ffn2_norm[l].reshape(1, d), w2_in, w2_out)

    st = lambda name: jnp.stack(outs[name])
    seq_last = lambda name: jnp.transpose(st(name), (0, 1, 4, 2, 3))
    return (yp.reshape(bp, lp, d), ys.reshape(bs, ls, d), seq_last("kp"), seq_last("vp"), st("fp"), st("sp"),
            st("ks"), st("vs"), st("fs"), st("ss"))
```

```python
import functools

import jax
import jax.numpy as jnp
import numpy as np
from jax import lax
from jax.experimental import pallas as pl
from jax.experimental.pallas import tpu as pltpu

F32 = jnp.float32
BF16 = jnp.bfloat16
EPS = 1e-6
NEG_INF = float("-inf")
LOG2E = 1.4426950408889634

V7X_LANES = 128
V7X_SUBLANES = 8
V7X_MXU_WIDTH = 256
V7X_VMEM_BYTES = 64 * 1024 * 1024
VMEM_LIMIT = (V7X_VMEM_BYTES * 3) // 4

ROW_TILE = 512
ATTN_TILE = 512
ATTN_UNROLL = 3
GLA_CHUNK = 64
GLA_LEAF = V7X_SUBLANES
GLA_ROWS = 512
PAGES_PER_STEP = 32


def _params(*sem):
    return pltpu.CompilerParams(dimension_semantics=sem, vmem_limit_bytes=VMEM_LIMIT)


def _resident(shape):
    nd = len(shape)
    return pl.BlockSpec(shape, lambda *_: (0,) * nd, pipeline_mode=pl.Buffered(1))


def _dot(a, b):
    return jnp.dot(a, b, preferred_element_type=F32)


def _dot_nt(a, b):
    return lax.dot_general(a, b, (((1,), (1,)), ((), ())), preferred_element_type=F32)


def _dot_tn(a, b):
    return lax.dot_general(a, b, (((0,), (0,)), ((), ())), preferred_element_type=F32)


def _split3(x):
    hi = x.astype(BF16)
    r1 = x - hi.astype(F32)
    mid = r1.astype(BF16)
    lo = (r1 - mid.astype(F32)).astype(BF16)
    return hi, mid, lo


def _dot01(m01, x):
    hi, mid, lo = _split3(x)
    return _dot(m01, hi) + _dot(m01, mid) + _dot(m01, lo)


def _dot01_nt(m01, x):
    hi, mid, lo = _split3(x)
    return _dot_nt(m01, hi) + _dot_nt(m01, mid) + _dot_nt(m01, lo)


def _dot01_r(x, m01):
    hi, mid, lo = _split3(x)
    return _dot(hi, m01) + _dot(mid, m01) + _dot(lo, m01)


def _group_sums(x, block_ones):
    w = block_ones.shape[0]
    return jnp.concatenate([_dot(x[:, i:i + w], block_ones) for i in range(0, x.shape[1], w)], axis=1)


def _rms_rows(x, gain):
    return x * lax.rsqrt(jnp.mean(x * x, axis=-1, keepdims=True) + EPS) * gain


def _sigmoid(x):
    return 1.0 / (1.0 + jnp.exp(-x))


def _ffn_body(x_ref, g_ref, wi_ref, wo_ref, o_ref, acc_ref, *, d_ff, tf):
    x = x_ref[...]
    h = _rms_rows(x, g_ref[...]).astype(BF16)
    for j in range(d_ff // tf):
        a = _dot(h, wi_ref[:, j * tf:(j + 1) * tf])
        b = _dot(h, wi_ref[:, d_ff + j * tf:d_ff + (j + 1) * tf])
        g = (a * _sigmoid(a) * b).astype(BF16)
        y = _dot(g, wo_ref[j * tf:(j + 1) * tf, :])
        if j == 0:
            acc_ref[...] = y
        else:
            acc_ref[...] += y
    o_ref[...] = x + 0.5 * acc_ref[...]


def _ffn(x, gain, w_in, w_out):
    n, d = x.shape
    d_ff = w_out.shape[0]
    tm = min(ROW_TILE, n)
    tf = 2 * V7X_LANES if d_ff % (2 * V7X_LANES) == 0 else V7X_LANES
    assert n % tm == 0 and d_ff % tf == 0
    return pl.pallas_call(
        functools.partial(_ffn_body, d_ff=d_ff, tf=tf),
        grid=(n // tm,),
        in_specs=[pl.BlockSpec((tm, d), lambda i: (i, 0)),
                  _resident((1, d)),
                  _resident((d, 2 * d_ff)),
                  _resident((d_ff, d))],
        out_specs=pl.BlockSpec((tm, d), lambda i: (i, 0)),
        out_shape=jax.ShapeDtypeStruct((n, d), F32),
        scratch_shapes=[pltpu.VMEM((tm, d), F32)],
        compiler_params=_params("parallel"),
        name="ffn",
    )(x, gain, w_in, w_out)


def _inproj_body(x_ref, g_ref, w_ref, wvt_ref, gsum_ref, qg_ref, kg_ref, fb_ref, lbl_ref,
                 q_ref, kf_ref, vf_ref, kb_ref, vb_ref, lf_ref,
                 hq_ref, hk_ref, hlf_ref, hv_ref, hg_ref, *, fw, hw, hvw, dh, layer, seq_major):
    h = _rms_rows(x_ref[...], g_ref[...]).astype(BF16)

    def proj(c0, n):
        return _dot(h, w_ref[:, c0:c0 + n])

    gsum = gsum_ref[...]

    def head_norm(z, gain):
        ms = _group_sums((z * z).astype(BF16), gsum) * (1.0 / dh)
        return z * lax.rsqrt(ms + EPS) * gain

    q = head_norm(proj(0, fw), qg_ref[...])
    q_ref[...] = (q * (dh ** -0.5 * LOG2E)).astype(BF16)
    k = head_norm(proj(fw, fw), kg_ref[...])
    kb_ref[...] = k.astype(BF16)
    if seq_major:
        v_t = _dot_nt(wvt_ref[...], h)
        kf_ref[...] = k.T
        vf_ref[...] = v_t
        vb_ref[...] = v_t.astype(BF16)
    else:
        v = proj(2 * fw, fw)
        kf_ref[...] = k
        vf_ref[...] = v
        vb_ref[...] = v.astype(BF16)

    c0 = 3 * fw
    hq = proj(c0, hw)
    hq_ref[...] = hq * _sigmoid(hq)

    lg = lbl_ref[...]
    e = jnp.exp(lg - jnp.max(lg, axis=0, keepdims=True))
    sm = e / jnp.sum(e, axis=0, keepdims=True)
    lb = jnp.zeros_like(sm[0:1])
    for i in range(1, layer + 1):
        lb = lb + sm[i:i + 1]
    log_lb = jnp.log(lb)
    log_1m_lb = jnp.log1p(-lb)

    z = proj(c0 + hw, hw)
    ez = jnp.exp(-jnp.abs(z))
    log_sig = jnp.minimum(z, 0.0) - jnp.log1p(ez)
    t = log_1m_lb + log_sig
    mx = jnp.maximum(log_lb, t)
    hlf_ref[...] = mx + jnp.log1p(jnp.exp(-jnp.abs(log_lb - t)))
    hk_ref[...] = (1.0 - lb) * (jnp.where(z >= 0.0, ez, 1.0) / (1.0 + ez))

    hv_ref[...] = proj(c0 + 2 * hw, hvw).astype(BF16)
    hg_ref[...] = proj(c0 + 2 * hw + hvw, hvw)

    zf = proj(c0 + 2 * hw + 2 * hvw, V7X_LANES) + fb_ref[...]
    lf_ref[...] = jnp.minimum(zf, 0.0) - jnp.log1p(jnp.exp(-jnp.abs(zf)))


def _inproj(x, gain, w_mix, w_vt, gsum, q_gain, k_gain, f_bias, lb_logits, *, fw, hw, hvw, dh, layer, seqs=None):
    n, d = x.shape
    tm = min(ROW_TILE, n)
    assert n % tm == 0
    row = lambda w: pl.BlockSpec((tm, w), lambda i: (i, 0))
    sds = lambda w, dt: jax.ShapeDtypeStruct((n, w), dt)
    if seqs is None:
        kv_specs = [row(fw), row(fw), row(fw)]
        kv_shapes = [sds(fw, F32), sds(fw, F32), sds(fw, BF16)]
    else:
        b, l = seqs
        nt = l // tm
        assert b * l == n and l % tm == 0
        kv_specs = [pl.BlockSpec((None, fw, tm), lambda i: (i // nt, 0, i % nt))] * 3
        kv_shapes = [jax.ShapeDtypeStruct((b, fw, l), dt) for dt in (F32, F32, BF16)]
    return pl.pallas_call(
        functools.partial(_inproj_body, fw=fw, hw=hw, hvw=hvw, dh=dh, layer=layer, seq_major=seqs is not None),
        grid=(n // tm,),
        in_specs=[row(d), _resident((1, d)), _resident(w_mix.shape), _resident(w_vt.shape), _resident(gsum.shape),
                  _resident((1, fw)), _resident((1, fw)), _resident((1, V7X_LANES)),
                  _resident(lb_logits.shape)],
        out_specs=[row(fw), kv_specs[0], kv_specs[1], row(fw), kv_specs[2], row(V7X_LANES),
                   row(hw), row(hw), row(hw), row(hvw), row(hvw)],
        out_shape=[sds(fw, BF16), kv_shapes[0], kv_shapes[1], sds(fw, BF16), kv_shapes[2],
                   sds(V7X_LANES, F32),
                   sds(hw, F32), sds(hw, F32), sds(hw, F32), sds(hvw, BF16), sds(hvw, F32)],
        compiler_params=_params("parallel"),
        name="inproj",
    )(x, gain, w_mix, w_vt, gsum, q_gain, k_gain, f_bias, lb_logits)


BIAS_PIECES = 3


def _bias_lane(hh, dh):
    return ((hh + 1) % (V7X_LANES // dh)) * dh


def _cumsum_body(lf_ref, k_ref, tril_ref, place_ref, c_ref, kaug_ref, carry_ref, *, nh, dh):
    @pl.when(pl.program_id(1) == 0)
    def _():
        carry_ref[...] = jnp.zeros_like(carry_ref)

    c = _dot01(tril_ref[...], lf_ref[...]) + carry_ref[...]
    c_ref[...] = c
    carry_ref[...] = c[-1:, :]
    pieces = jnp.concatenate(_split3(-LOG2E * c), axis=1)
    lane = lax.broadcasted_iota(jnp.int32, (1, V7X_LANES), 1)
    hp = V7X_LANES // dh
    placed = _dot(pieces, place_ref[...]).astype(BF16)
    for h in range(nh):
        tile = h // hp
        keys = k_ref[:, tile * V7X_LANES:(tile + 1) * V7X_LANES]
        kaug_ref[h] = jnp.where((lane // dh) == (h % hp), keys, placed[:, h * V7X_LANES:(h + 1) * V7X_LANES])


def _cumsum(lf, k, tril, *, nh, dh):
    b, l, w = lf.shape
    tc = tril.shape[0]
    nt = l // tc
    place = np.zeros((BIAS_PIECES * w, nh * V7X_LANES), np.float32)
    for h in range(nh):
        for piece in range(BIAS_PIECES):
            place[piece * w + h, h * V7X_LANES + _bias_lane(h % (V7X_LANES // dh), dh) + piece] = 1.0
    place = jnp.asarray(place, BF16)
    return pl.pallas_call(
        functools.partial(_cumsum_body, nh=nh, dh=dh),
        grid=(b, nt),
        in_specs=[pl.BlockSpec((None, tc, w), lambda i, j: (i, j, 0)),
                  pl.BlockSpec((tc, k.shape[1]), lambda i, j: (i * nt + j, 0)),
                  _resident(tril.shape), _resident(place.shape)],
        out_specs=[pl.BlockSpec((None, tc, w), lambda i, j: (i, j, 0)),
                   pl.BlockSpec((None, nh, tc, V7X_LANES), lambda i, j: (i, 0, j, 0))],
        out_shape=[jax.ShapeDtypeStruct((b, l, w), F32), jax.ShapeDtypeStruct((b, nh, l, V7X_LANES), BF16)],
        scratch_shapes=[pltpu.VMEM((1, w), F32)],
        compiler_params=_params("parallel", "arbitrary"),
        name="logf_cumsum",
    )(lf, k, tril, place)


def _attn_body(q_ref, kaug_ref, vt_ref, crow_ref, o_ref, m_ref, acc_ref, u_ref, *, t, dh):
    p = pl.program_id(1)
    qi = pl.program_id(2)
    hp = V7X_LANES // dh
    q = q_ref[...]
    lane = lax.broadcasted_iota(jnp.int32, (1, V7X_LANES), 1)
    key_i = lax.broadcasted_iota(jnp.int32, (t, t), 0)
    qry_i = lax.broadcasted_iota(jnp.int32, (t, t), 1)
    feat = lax.broadcasted_iota(jnp.int32, (V7X_LANES, t), 0)
    qas, cts, keep, ones = [], [], [], []
    for hh in range(hp):
        b0 = _bias_lane(hh, dh)
        one_lanes = jnp.where((lane >= b0) & (lane < b0 + BIAS_PIECES), 1.0, 0.0).astype(BF16)
        qas.append(jnp.where((lane // dh) == hh, q, one_lanes))
        cts.append(LOG2E * crow_ref[pl.ds(p * hp + hh, 1), :])
        keep.append(jnp.where((feat // dh) == hh, 1.0, 0.0).astype(BF16))
        ones.append(jnp.where(feat == b0, 1.0, 0.0).astype(BF16))
        m_ref[hh] = jnp.full(m_ref.shape[1:], NEG_INF, F32)
        acc_ref[hh] = jnp.zeros(acc_ref.shape[1:], F32)

    heads = range(hp)

    def scores(kb):
        r0 = pl.multiple_of(kb * t, t)
        return [_dot_nt(kaug_ref[hh, pl.ds(r0, t), :], qas[hh]) for hh in heads]

    def prefetch_scores(kb, slot):
        for hh, u in enumerate(scores(kb)):
            u_ref[slot, hh] = u

    def steps(group, kbs, diagonal_last):
        slot = group % 2
        r0s = [pl.multiple_of(kb * t, t) for kb in kbs]
        us = [[u_ref[slot, hh] for hh in heads]] + [scores(kb) for kb in kbs[1:]]
        if not diagonal_last:
            prefetch_scores(kbs[-1] + 1, 1 - slot)
        for n, (r0, u_heads) in enumerate(zip(r0s, us)):
            vt = vt_ref[:, pl.ds(r0, t)]
            if diagonal_last and n == len(kbs) - 1:
                u_heads = [jnp.where(key_i <= qry_i, u, NEG_INF) for u in u_heads]
            m_old = [m_ref[hh] for hh in heads]
            m_new = [jnp.maximum(m_old[hh], jnp.max(u_heads[hh], axis=0, keepdims=True) + cts[hh]) for hh in heads]
            pe = [jnp.exp2(u_heads[hh] - (m_new[hh] - cts[hh])).astype(BF16) for hh in heads]
            pv = [_dot(vt * keep[hh] + ones[hh], pe[hh]) for hh in heads]
            for hh in heads:
                acc_ref[hh] = jnp.exp2(m_old[hh] - m_new[hh]) * acc_ref[hh] + pv[hh]
                m_ref[hh] = m_new[hh]

    def body(i, carry):
        steps(i, [ATTN_UNROLL * i + r for r in range(ATTN_UNROLL)], False)
        return carry

    prefetch_scores(0, 0)
    lax.fori_loop(0, qi // ATTN_UNROLL, body, 0)
    for rem in range(ATTN_UNROLL):
        @pl.when(qi % ATTN_UNROLL == rem)
        def _(rem=rem):
            steps(qi // ATTN_UNROLL, [qi - rem + r for r in range(rem)] + [qi], True)
    o_t = None
    for hh in range(hp):
        b0 = _bias_lane(hh, dh)
        acc = acc_ref[hh]
        o_h = acc / acc[b0:b0 + 1, :]
        o_t = o_h if o_t is None else jnp.where((feat // dh) == hh, o_h, o_t)
    o_ref[...] = o_t.T.astype(o_ref.dtype)


def _attn_prompt(q, kaug, vt, crow, *, b, l, dh):
    n, fw = q.shape
    t = min(ATTN_TILE, l)
    nq = l // t
    hp = V7X_LANES // dh
    assert l % t == 0 and fw % V7X_LANES == 0 and V7X_LANES % dh == 0 and BIAS_PIECES < dh
    return pl.pallas_call(
        functools.partial(_attn_body, t=t, dh=dh),
        grid=(b, fw // V7X_LANES, nq),
        in_specs=[pl.BlockSpec((t, V7X_LANES), lambda i, p, j: (i * nq + j, p)),
                  pl.BlockSpec((None, hp, l, V7X_LANES), lambda i, p, j: (i, p, 0, 0)),
                  pl.BlockSpec((None, V7X_LANES, l), lambda i, p, j: (i, p, 0)),
                  pl.BlockSpec((None, crow.shape[1], t), lambda i, p, j: (i, 0, j))],
        out_specs=pl.BlockSpec((t, V7X_LANES), lambda i, p, j: (i * nq + j, p)),
        out_shape=jax.ShapeDtypeStruct((n, fw), BF16),
        scratch_shapes=[pltpu.VMEM((hp, 1, t), F32), pltpu.VMEM((hp, V7X_LANES, t), F32),
                        pltpu.VMEM((2, hp, t, t), F32)],
        compiler_params=_params("parallel", "parallel", "arbitrary"),
        name="fox_prompt",
    )(q, kaug, vt, crow)


def _gla_masks(c):
    r = np.arange(c)[:, None]
    s = np.arange(c)[None, :]
    tril = (s <= r).astype(np.float32)
    levels = []
    m = c // 2
    while m >= GLA_LEAF:
        levels.append((((r ^ s) < 2 * m) & ((r & m) != 0) & ((s & m) == 0)).astype(np.float32))
        m //= 2
    if not levels:
        levels.append(np.zeros((c, c), np.float32))
    return jnp.asarray(tril, BF16), jnp.asarray(np.stack(levels), F32)


def _gla_chunk(q, k, lf, v, states, tril, lvl_ref, leaf_refs, *, c, nh, kd, vd):
    def head(x, h, w):
        return x[:, h * w:(h + 1) * w]

    b = _dot01(tril, LOG2E * lf)
    q = q * (kd ** -0.5)
    b_last = b[c - 1:c, :]
    q_hat = (q * jnp.exp2(b)).astype(BF16)
    k_hat = (k * jnp.exp2(b_last - b)).astype(BF16)
    decay_last = jnp.exp2(b_last)
    outs = [_dot_nt(head(q_hat, h, kd), states[h].astype(BF16)) for h in range(nh)]
    new_states = [states[h] * head(decay_last, h, kd) + _dot_tn(head(v, h, vd), head(k_hat, h, kd))
                  for h in range(nh)]

    m = c // 2
    lvl = 0
    scores = [None] * nh
    while m >= GLA_LEAF:
        pieces = []
        for blk in range(c // (2 * m)):
            mid = blk * 2 * m + m
            pieces.append(jnp.broadcast_to(b[mid - 1:mid, :], (2 * m, b.shape[1])))
        ref = pieces[0] if len(pieces) == 1 else jnp.concatenate(pieces, axis=0)
        e = jnp.exp2(-jnp.abs(b - ref))
        qe = (q * e).astype(BF16)
        ke = (k * e).astype(BF16)
        mask = lvl_ref[lvl]
        for h in range(nh):
            part = mask * _dot_nt(head(qe, h, kd), head(ke, h, kd))
            scores[h] = part if scores[h] is None else scores[h] + part
        m //= 2
        lvl += 1
    if scores[0] is not None:
        outs = [outs[h] + _dot(scores[h].astype(BF16), head(v, h, vd)) for h in range(nh)]

    nl = c // GLA_LEAF
    b_ref, q_ref, k_ref, v_ref, o_ref = leaf_refs
    v32 = v.astype(F32)
    for h in range(nh):
        b_ref[h] = head(b, h, kd)
        q_ref[h] = head(q, h, kd)
        k_ref[h] = head(k, h, kd)
        v_ref[h] = head(v32, h, vd)

    def rows(ref, i):
        return jnp.concatenate([ref[h, pl.ds(i, nl, stride=GLA_LEAF), :] for h in range(nh)], axis=1)

    bs = [rows(b_ref, i) for i in range(GLA_LEAF)]
    ks = [rows(k_ref, i) for i in range(GLA_LEAF)]
    vs = [rows(v_ref, i) for i in range(GLA_LEAF)]
    for t in range(GLA_LEAF):
        q_t = rows(q_ref, t)
        o_t = [None] * nh
        for s in range(t + 1):
            prod = jnp.exp2(bs[t] - bs[s]) * q_t * ks[s]
            for h in range(nh):
                w = jnp.sum(head(prod, h, kd), axis=-1, keepdims=True)
                term = w * head(vs[s], h, vd)
                o_t[h] = term if o_t[h] is None else o_t[h] + term
        for h in range(nh):
            o_ref[h, pl.ds(t, nl, stride=GLA_LEAF), :] = o_t[h]
    return jnp.concatenate([outs[h] + o_ref[h] for h in range(nh)], axis=1), new_states


def _gla_body(q_ref, k_ref, lf_ref, v_ref, s0_ref, tril_ref, lvl_ref, o_ref, sT_ref, st_ref, *leaf_refs,
              c, rows, nh, kd, vd):
    j = pl.program_id(1)

    @pl.when(j == 0)
    def _():
        st_ref[...] = s0_ref[...]

    tril = tril_ref[...]

    def chunk(ci, carry):
        rs = pl.ds(pl.multiple_of(ci * c, c), c)
        o, new_states = _gla_chunk(q_ref[rs, :], k_ref[rs, :], lf_ref[rs, :], v_ref[rs, :],
                                   [st_ref[h] for h in range(nh)], tril, lvl_ref, leaf_refs,
                                   c=c, nh=nh, kd=kd, vd=vd)
        o_ref[rs, :] = o
        for h in range(nh):
            st_ref[h] = new_states[h]
        return carry

    if rows == c:
        chunk(0, 0)
    else:
        lax.fori_loop(0, rows // c, chunk, 0, unroll=True)

    @pl.when(j == pl.num_programs(1) - 1)
    def _():
        sT_ref[...] = st_ref[...]


def _gla(q, k, lf, v, s0_t, *, b, l):
    n, hw = q.shape
    hvw = v.shape[1]
    nh, vd, kd = s0_t.shape[1:]
    c = min(GLA_CHUNK, l)
    rows = min(GLA_ROWS, l)
    assert l % rows == 0 and rows % c == 0 and c % GLA_LEAF == 0
    tril, lvls = _gla_masks(c)
    nr = l // rows
    row = lambda w: pl.BlockSpec((rows, w), lambda i, j: (i * nr + j, 0))
    state = pl.BlockSpec((None, nh, vd, kd), lambda i, j: (i, 0, 0, 0))
    return pl.pallas_call(
        functools.partial(_gla_body, c=c, rows=rows, nh=nh, kd=kd, vd=vd),
        grid=(b, nr),
        in_specs=[row(hw), row(hw), row(hw), row(hvw), state, _resident(tril.shape), _resident(lvls.shape)],
        out_specs=[row(hvw), state],
        out_shape=[jax.ShapeDtypeStruct((n, hvw), F32), jax.ShapeDtypeStruct(s0_t.shape, F32)],
        scratch_shapes=[pltpu.VMEM((nh, vd, kd), F32)]
        + [pltpu.VMEM((nh, c, w), F32) for w in (kd, kd, kd, vd, vd)],
        compiler_params=_params("parallel", "arbitrary"),
        name="hgrn2",
    )(q, k, lf, v, s0_t, tril, lvls)


def _merge_body(x_ref, g_ref, wg_ref, of_ref, oh_ref, hg_ref, og_ref, gsum_ref, wpf_ref, wph_ref, wo_ref,
                o_ref, *, d, vd):
    x = x_ref[...]
    h = _rms_rows(x, g_ref[...]).astype(BF16)
    gate_fox = _sigmoid(_dot(h, wg_ref[:, :d]))
    gate_hg = _sigmoid(_dot(h, wg_ref[:, d:]))
    y_fox = _dot(of_ref[...], wpf_ref[...])
    oh = oh_ref[...]
    ms = _group_sums((oh * oh).astype(BF16), gsum_ref[...]) * (1.0 / vd)
    hg = hg_ref[...]
    ob = (oh * lax.rsqrt(ms + EPS) * og_ref[...]) * (hg * _sigmoid(hg))
    y_hg = _dot(ob.astype(BF16), wph_ref[...])
    y = gate_fox * y_fox + gate_hg * y_hg
    o_ref[...] = x + _dot(y.astype(BF16), wo_ref[...])


def _merge(x, gain, w_gate, o_fox, o_hg, hgate, o_gain, gsum, w_pf, w_ph, w_out, *, vd):
    n, d = x.shape
    tm = min(ROW_TILE, n)
    assert n % tm == 0
    row = lambda w: pl.BlockSpec((tm, w), lambda i: (i, 0))
    return pl.pallas_call(
        functools.partial(_merge_body, d=d, vd=vd),
        grid=(n // tm,),
        in_specs=[row(d), _resident((1, d)), _resident(w_gate.shape), row(o_fox.shape[1]), row(o_hg.shape[1]),
                  row(hgate.shape[1]), _resident(o_gain.shape), _resident(gsum.shape),
                  _resident(w_pf.shape), _resident(w_ph.shape), _resident(w_out.shape)],
        out_specs=row(d),
        out_shape=jax.ShapeDtypeStruct((n, d), F32),
        compiler_params=_params("parallel"),
        name="merge",
    )(x, gain, w_gate, o_fox, o_hg, hgate, o_gain, gsum, w_pf, w_ph, w_out)


def _paged_body(pt_ref, *refs, g, ps, nh, dh, t):
    k_refs = refs[:g]
    v_refs = refs[g:2 * g]
    lf_refs = refs[2 * g:3 * g]
    (w_ref, q_ref, kn_ref, vn_ref, lfn_ref, o_ref,
     qbd_ref, ctcol_ref, carry_ref, m_ref, l_ref, acc_ref) = refs[3 * g:]
    j = pl.program_id(1)
    rows = nh * t
    fw = nh * dh
    row_head = lax.broadcasted_iota(jnp.int32, (rows, 1), 0) // t

    def new_token_cumsum():
        lfn = lfn_ref[...]
        sub = lax.broadcasted_iota(jnp.int32, lfn.shape, 0)
        ct = jnp.zeros_like(lfn)
        for s in range(t):
            ct = ct + jnp.where(sub >= s, lfn[s:s + 1, :], 0.0)
        return ct

    @pl.when(j == 0)
    def _():
        lane_head = lax.broadcasted_iota(jnp.int32, (rows, fw), 1) // dh
        q_rep = jnp.concatenate([q_ref[...]] * nh, axis=0)
        qbd_ref[...] = jnp.where(lane_head == row_head, q_rep, jnp.zeros_like(q_rep))
        ct_rep = jnp.concatenate([new_token_cumsum()] * nh, axis=0)
        lane = lax.broadcasted_iota(jnp.int32, ct_rep.shape, 1)
        ctcol_ref[...] = LOG2E * jnp.sum(jnp.where(lane == row_head, ct_rep, 0.0), axis=-1, keepdims=True)
        carry_ref[...] = jnp.zeros_like(carry_ref)
        m_ref[...] = jnp.full_like(m_ref, NEG_INF)
        l_ref[...] = jnp.zeros_like(l_ref)
        acc_ref[...] = jnp.zeros_like(acc_ref)

    qbd = qbd_ref[...]
    ctcol = ctcol_ref[...]

    def update(u, pv_fn):
        m_old = m_ref[...]
        m_new = jnp.maximum(m_old, jnp.max(u, axis=-1, keepdims=True) + ctcol)
        pe = jnp.exp2(u - (m_new - ctcol))
        alpha = jnp.exp2(m_old - m_new)
        l_ref[...] = alpha * l_ref[...] + jnp.sum(pe, axis=-1, keepdims=True)
        acc_ref[...] = alpha * acc_ref[...] + pv_fn(pe)
        m_ref[...] = m_new

    lf_all = jnp.concatenate([lf_refs[i][...] for i in range(g)], axis=0)
    sums = _dot01_r(lf_all, w_ref[...])
    carry = carry_ref[...]
    biases = [None] * g
    for i in reversed(range(g)):
        biases[i] = sums[i * nh:(i + 1) * nh, :ps] + carry
        carry = carry + sums[i * nh:(i + 1) * nh, ps:]
    carry_ref[...] = carry
    bias = LOG2E * jnp.concatenate(biases, axis=1)
    bias = jnp.broadcast_to(bias[:, None, :], (nh, t, g * ps)).reshape(rows, g * ps)

    s = jnp.concatenate([_dot(qbd, k_refs[i][...].astype(BF16)) for i in range(g)], axis=1)

    def pv_past(pe):
        out = None
        for i in range(g):
            part = _dot_nt(pe[:, i * ps:(i + 1) * ps].astype(BF16), v_refs[i][...].astype(BF16))
            out = part if out is None else out + part
        return out

    update(s + bias, pv_past)

    @pl.when(j == pl.num_programs(1) - 1)
    def _():
        pad = jnp.zeros((ps - t, fw), BF16)
        kn = jnp.concatenate([kn_ref[...], pad], axis=0)
        vn = jnp.concatenate([vn_ref[...], pad], axis=0)
        ct = new_token_cumsum()
        ct_pad = jnp.concatenate([ct, jnp.zeros((ps - t, ct.shape[1]), F32)], axis=0)
        lane = lax.broadcasted_iota(jnp.int32, (rows, ct.shape[1]), 1)
        onehot = jnp.where(lane == row_head, 1.0, 0.0).astype(BF16)
        c_s = _dot01_nt(onehot, ct_pad)
        col = lax.broadcasted_iota(jnp.int32, (rows, ps), 1)
        row_t = lax.broadcasted_iota(jnp.int32, (rows, ps), 0) % t
        u = jnp.where(col <= row_t, _dot_nt(qbd, kn) - LOG2E * c_s, NEG_INF)
        update(u, lambda pe: _dot(pe.astype(BF16), vn))
        o = acc_ref[...] / l_ref[...]
        lane_head = lax.broadcasted_iota(jnp.int32, (rows, fw), 1) // dh
        o = jnp.where(lane_head == row_head, o, 0.0)
        o_ref[...] = jnp.sum(o.reshape(nh, t, fw), axis=0).astype(o_ref.dtype)


def _attn_paged(cache_kt, cache_vt, cache_lft, page_table, q, k_new, v_new, lf_new, *, layer, t, dh):
    _, _, fw, ps = cache_kt.shape
    nb, n_pages = page_table.shape
    nh = fw // dh
    g = min(PAGES_PER_STEP, n_pages)
    ng = n_pages // g
    rows = nh * t
    assert n_pages % g == 0 and t <= ps and cache_lft.shape[2:] == (nh, ps)
    j_idx = np.arange(ps)
    w = np.concatenate([(j_idx[:, None] > j_idx[None, :]), np.ones((ps, ps), bool)], axis=1)
    w = jnp.asarray(w.astype(np.float32), BF16)

    def page(i, width):
        return pl.BlockSpec((None, None, width, ps),
                            lambda b, j, pt: (layer, pt[b, (ng - 1 - j) * g + i], 0, 0))

    new_rows = lambda width: pl.BlockSpec((t, width), lambda b, j, pt: (b, 0))
    grid_spec = pltpu.PrefetchScalarGridSpec(
        num_scalar_prefetch=1,
        grid=(nb, ng),
        in_specs=[page(i, fw) for i in range(g)] + [page(i, fw) for i in range(g)]
        + [page(i, nh) for i in range(g)]
        + [pl.BlockSpec(w.shape, lambda b, j, pt: (0, 0)),
           new_rows(fw), new_rows(fw), new_rows(fw), new_rows(lf_new.shape[1])],
        out_specs=new_rows(fw),
        scratch_shapes=[pltpu.VMEM((rows, fw), BF16), pltpu.VMEM((rows, 1), F32), pltpu.VMEM((nh, ps), F32),
                        pltpu.VMEM((rows, 1), F32), pltpu.VMEM((rows, 1), F32), pltpu.VMEM((rows, fw), F32)],
    )
    return pl.pallas_call(
        functools.partial(_paged_body, g=g, ps=ps, nh=nh, dh=dh, t=t),
        grid_spec=grid_spec,
        out_shape=jax.ShapeDtypeStruct((nb * t, fw), BF16),
        compiler_params=_params("parallel", "arbitrary"),
        name="fox_paged",
    )(page_table, *([cache_kt] * g), *([cache_vt] * g), *([cache_lft] * g), w, q, k_new, v_new, lf_new)


def _block_ones(width, group):
    i = np.arange(width) // group
    return jnp.asarray((i[:, None] == i[None, :]).astype(np.float32), BF16)


def kernel(x_prompt, x_sample, cache_k, cache_v, cache_logf, state_hgrn, page_table, ffn1_norm, ffn1_w_in, ffn1_w_out, mix_norm, w_in, fox_f_bias, fox_q_gain, fox_k_gain, hg_lb_logits, hg_o_gain, w_proj_fox, w_proj_hg, w_out, ffn2_norm, ffn2_w_in, ffn2_w_out):
    depth = w_in.shape[0]
    bp, lp, d = x_prompt.shape
    bs, ls, _ = x_sample.shape
    nh, dh = cache_k.shape[-2:]
    fw = nh * dh
    hg_heads, kd, vd = state_hgrn.shape[-3:]
    hw, hvw = hg_heads * kd, hg_heads * vd
    assert nh <= V7X_LANES and w_in.shape[-1] == 3 * fw + nh + 2 * hw + 2 * hvw + 2 * d

    assert fw % V7X_MXU_WIDTH == 0 and hvw % V7X_MXU_WIDTH == 0
    gsum_fox = _block_ones(V7X_MXU_WIDTH, dh)
    gsum_hg = _block_ones(V7X_MXU_WIDTH, vd)
    tc = min(ROW_TILE, lp)
    tril_seq = jnp.asarray(np.tril(np.ones((tc, tc), np.float32)), BF16)
    lb_logits = hg_lb_logits.astype(F32)
    cache_kt = jnp.transpose(cache_k, (0, 1, 3, 4, 2)).reshape(cache_k.shape[:2] + (fw, cache_k.shape[2]))
    cache_vt = jnp.transpose(cache_v, (0, 1, 3, 4, 2)).reshape(cache_v.shape[:2] + (fw, cache_v.shape[2]))
    cache_lft = jnp.transpose(cache_logf, (0, 1, 3, 2)).astype(F32)

    yp = x_prompt.reshape(bp * lp, d)
    ys = x_sample.reshape(bs * ls, d)
    outs = {name: [] for name in ("kp", "vp", "fp", "sp", "ks", "vs", "fs", "ss")}

    for l in range(depth):
        c_mix = 3 * fw + nh
        w_l = w_in[l]
        w_mix = jnp.concatenate(
            [w_l[:, :3 * fw], w_l[:, c_mix:c_mix + 2 * hw + 2 * hvw],
             jnp.pad(w_l[:, 3 * fw:c_mix], ((0, 0), (0, V7X_LANES - nh)))], axis=1).astype(BF16)
        w_gate = w_l[:, c_mix + 2 * hw + 2 * hvw:].astype(BF16)
        f_bias = jnp.pad(fox_f_bias[l], (0, V7X_LANES - nh)).reshape(1, V7X_LANES)
        q_gain = jnp.tile(fox_q_gain[l], nh).reshape(1, fw)
        k_gain = jnp.tile(fox_k_gain[l], nh).reshape(1, fw)
        o_gain = jnp.tile(hg_o_gain[l], hg_heads).reshape(1, hvw)
        mix_gain = mix_norm[l].reshape(1, d)
        w1_in, w1_out = ffn1_w_in[l].astype(BF16), ffn1_w_out[l].astype(BF16)
        w2_in, w2_out = ffn2_w_in[l].astype(BF16), ffn2_w_out[l].astype(BF16)
        w_pf, w_ph, w_o = w_proj_fox[l].astype(BF16), w_proj_hg[l].astype(BF16), w_out[l].astype(BF16)

        yp = _ffn(yp, ffn1_norm[l].reshape(1, d), w1_in, w1_out)
        ys = _ffn(ys, ffn1_norm[l].reshape(1, d), w1_in, w1_out)

        w_vt = jnp.transpose(w_l[:, 2 * fw:3 * fw]).astype(BF16)
        inproj = functools.partial(_inproj, gain=mix_gain, w_mix=w_mix, w_vt=w_vt, gsum=gsum_fox, q_gain=q_gain,
                                   k_gain=k_gain, f_bias=f_bias, lb_logits=lb_logits,
                                   fw=fw, hw=hw, hvw=hvw, dh=dh, layer=l)
        merge = functools.partial(_merge, gain=mix_gain, w_gate=w_gate, o_gain=o_gain, gsum=gsum_hg,
                                  w_pf=w_pf, w_ph=w_ph, w_out=w_o, vd=vd)

        q, kf, vf, kb, v_t, lf, hq, hk, hlf, hv, hgate = inproj(yp, seqs=(bp, lp))
        c, kaug = _cumsum(lf.reshape(bp, lp, V7X_LANES), kb, tril_seq, nh=nh, dh=dh)
        c_row = jnp.transpose(c[:, :, :nh], (0, 2, 1))
        o_fox = _attn_prompt(q, kaug, v_t, c_row, b=bp, l=lp, dh=dh)
        s0_t = jnp.zeros((bp, hg_heads, vd, kd), F32)
        o_hg, s_t = _gla(hq, hk, hlf, hv, s0_t, b=bp, l=lp)
        yp = merge(yp, o_fox=o_fox, o_hg=o_hg, hgate=hgate)
        outs["kp"].append(kf.reshape(bp, nh, dh, lp))
        outs["vp"].append(vf.reshape(bp, nh, dh, lp))
        outs["fp"].append(lf[:, :nh].reshape(bp, lp, nh).astype(cache_logf.dtype))
        outs["sp"].append(jnp.swapaxes(s_t, -1, -2).astype(x_prompt.dtype))

        q, kf, vf, kb, vb, lf, hq, hk, hlf, hv, hgate = inproj(ys)
        o_fox = _attn_paged(cache_kt, cache_vt, cache_lft, page_table, q, kb, vb, lf, layer=l, t=ls, dh=dh)
        s0_t = jnp.swapaxes(state_hgrn[l].astype(F32), -1, -2)
        o_hg, s_t = _gla(hq, hk, hlf, hv, s0_t, b=bs, l=ls)
        ys = merge(ys, o_fox=o_fox, o_hg=o_hg, hgate=hgate)
        outs["ks"].append(kf.reshape(bs, ls, nh, dh))
        outs["vs"].append(vf.reshape(bs, ls, nh, dh))
        outs["fs"].append(lf[:, :nh].reshape(bs, ls, nh).astype(cache_logf.dtype))
        outs["ss"].append(jnp.swapaxes(s_t, -1, -2).astype(state_hgrn.dtype))

        yp = _ffn(yp, ffn2_norm[l].reshape(1, d), w2_in, w2_out)
        ys = _ffn(ys, ffn2_norm[l].reshape(1, d), w2_in, w2_out)

    st = lambda name: jnp.stack(outs[name])
    seq_last = lambda name: jnp.transpose(st(name), (0, 1, 4, 2, 3))
    return (yp.reshape(bp, lp, d), ys.reshape(bs, ls, d), seq_last("kp"), seq_last("vp"), st("fp"), st("sp"),
            st("ks"), st("vs"), st("fs"), st("ss"))
```

```python
import functools

import jax
import jax.numpy as jnp
import numpy as np
from jax import lax
from jax.experimental import pallas as pl
from jax.experimental.pallas import tpu as pltpu

F32 = jnp.float32
BF16 = jnp.bfloat16
EPS = 1e-6
NEG_INF = float("-inf")
LOG2E = 1.4426950408889634

V7X_LANES = 128
V7X_SUBLANES = 8
V7X_MXU_WIDTH = 256
V7X_VMEM_BYTES = 64 * 1024 * 1024
VMEM_LIMIT = (V7X_VMEM_BYTES * 3) // 4

ROW_TILE = 512
ATTN_TILE = 512
ATTN_UNROLL = 3
GLA_CHUNK = 128
GLA_LEAF = V7X_SUBLANES
GLA_ROWS = 512
PAGES_PER_STEP = 32


def _params(*sem):
    return pltpu.CompilerParams(dimension_semantics=sem, vmem_limit_bytes=VMEM_LIMIT)


def _resident(shape):
    nd = len(shape)
    return pl.BlockSpec(shape, lambda *_: (0,) * nd, pipeline_mode=pl.Buffered(1))


def _dot(a, b):
    return jnp.dot(a, b, preferred_element_type=F32)


def _dot_nt(a, b):
    return lax.dot_general(a, b, (((1,), (1,)), ((), ())), preferred_element_type=F32)


def _dot_tn(a, b):
    return lax.dot_general(a, b, (((0,), (0,)), ((), ())), preferred_element_type=F32)


def _split3(x):
    hi = x.astype(BF16)
    r1 = x - hi.astype(F32)
    mid = r1.astype(BF16)
    lo = (r1 - mid.astype(F32)).astype(BF16)
    return hi, mid, lo


def _dot01(m01, x):
    hi, mid, lo = _split3(x)
    return _dot(m01, hi) + _dot(m01, mid) + _dot(m01, lo)


def _dot01_nt(m01, x):
    hi, mid, lo = _split3(x)
    return _dot_nt(m01, hi) + _dot_nt(m01, mid) + _dot_nt(m01, lo)


def _dot01_r(x, m01):
    hi, mid, lo = _split3(x)
    return _dot(hi, m01) + _dot(mid, m01) + _dot(lo, m01)


def _group_sums(x, block_ones):
    w = block_ones.shape[0]
    return jnp.concatenate([_dot(x[:, i:i + w], block_ones) for i in range(0, x.shape[1], w)], axis=1)


def _rms_rows(x, gain):
    return x * lax.rsqrt(jnp.mean(x * x, axis=-1, keepdims=True) + EPS) * gain


def _sigmoid(x):
    return 1.0 / (1.0 + jnp.exp(-x))


def _ffn_body(x_ref, g_ref, wi_ref, wo_ref, o_ref, acc_ref, *, d_ff, tf):
    x = x_ref[...]
    h = _rms_rows(x, g_ref[...]).astype(BF16)
    for j in range(d_ff // tf):
        a = _dot(h, wi_ref[:, j * tf:(j + 1) * tf])
        b = _dot(h, wi_ref[:, d_ff + j * tf:d_ff + (j + 1) * tf])
        g = (a * _sigmoid(a) * b).astype(BF16)
        y = _dot(g, wo_ref[j * tf:(j + 1) * tf, :])
        if j == 0:
            acc_ref[...] = y
        else:
            acc_ref[...] += y
    o_ref[...] = x + 0.5 * acc_ref[...]


def _ffn(x, gain, w_in, w_out):
    n, d = x.shape
    d_ff = w_out.shape[0]
    tm = min(ROW_TILE, n)
    tf = 2 * V7X_LANES if d_ff % (2 * V7X_LANES) == 0 else V7X_LANES
    assert n % tm == 0 and d_ff % tf == 0
    return pl.pallas_call(
        functools.partial(_ffn_body, d_ff=d_ff, tf=tf),
        grid=(n // tm,),
        in_specs=[pl.BlockSpec((tm, d), lambda i: (i, 0)),
                  _resident((1, d)),
                  _resident((d, 2 * d_ff)),
                  _resident((d_ff, d))],
        out_specs=pl.BlockSpec((tm, d), lambda i: (i, 0)),
        out_shape=jax.ShapeDtypeStruct((n, d), F32),
        scratch_shapes=[pltpu.VMEM((tm, d), F32)],
        compiler_params=_params("parallel"),
        name="ffn",
    )(x, gain, w_in, w_out)


def _inproj_body(x_ref, g_ref, w_ref, wvt_ref, gsum_ref, qg_ref, kg_ref, fb_ref, lbl_ref,
                 q_ref, kf_ref, vf_ref, kb_ref, vb_ref, lf_ref,
                 hq_ref, hk_ref, hlf_ref, hv_ref, hg_ref, *, fw, hw, hvw, dh, layer, seq_major):
    h = _rms_rows(x_ref[...], g_ref[...]).astype(BF16)

    def proj(c0, n):
        return _dot(h, w_ref[:, c0:c0 + n])

    gsum = gsum_ref[...]

    def head_norm(z, gain):
        ms = _group_sums((z * z).astype(BF16), gsum) * (1.0 / dh)
        return z * lax.rsqrt(ms + EPS) * gain

    q = head_norm(proj(0, fw), qg_ref[...])
    q_ref[...] = (q * (dh ** -0.5 * LOG2E)).astype(BF16)
    k = head_norm(proj(fw, fw), kg_ref[...])
    kb_ref[...] = k.astype(BF16)
    if seq_major:
        v_t = _dot_nt(wvt_ref[...], h)
        kf_ref[...] = k.T
        vf_ref[...] = v_t
        vb_ref[...] = v_t.astype(BF16)
    else:
        v = proj(2 * fw, fw)
        kf_ref[...] = k
        vf_ref[...] = v
        vb_ref[...] = v.astype(BF16)

    c0 = 3 * fw
    hq = proj(c0, hw)
    hq_ref[...] = hq * _sigmoid(hq)

    lg = lbl_ref[...]
    e = jnp.exp(lg - jnp.max(lg, axis=0, keepdims=True))
    sm = e / jnp.sum(e, axis=0, keepdims=True)
    lb = jnp.zeros_like(sm[0:1])
    for i in range(1, layer + 1):
        lb = lb + sm[i:i + 1]
    log_lb = jnp.log(lb)
    log_1m_lb = jnp.log1p(-lb)

    z = proj(c0 + hw, hw)
    ez = jnp.exp(-jnp.abs(z))
    log_sig = jnp.minimum(z, 0.0) - jnp.log1p(ez)
    t = log_1m_lb + log_sig
    mx = jnp.maximum(log_lb, t)
    hlf_ref[...] = mx + jnp.log1p(jnp.exp(-jnp.abs(log_lb - t)))
    hk_ref[...] = (1.0 - lb) * (jnp.where(z >= 0.0, ez, 1.0) / (1.0 + ez))

    hv_ref[...] = proj(c0 + 2 * hw, hvw).astype(BF16)
    hg_ref[...] = proj(c0 + 2 * hw + hvw, hvw)

    zf = proj(c0 + 2 * hw + 2 * hvw, V7X_LANES) + fb_ref[...]
    lf_ref[...] = jnp.minimum(zf, 0.0) - jnp.log1p(jnp.exp(-jnp.abs(zf)))


def _inproj(x, gain, w_mix, w_vt, gsum, q_gain, k_gain, f_bias, lb_logits, *, fw, hw, hvw, dh, layer, seqs=None):
    n, d = x.shape
    tm = min(ROW_TILE, n)
    assert n % tm == 0
    row = lambda w: pl.BlockSpec((tm, w), lambda i: (i, 0))
    sds = lambda w, dt: jax.ShapeDtypeStruct((n, w), dt)
    if seqs is None:
        kv_specs = [row(fw), row(fw), row(fw)]
        kv_shapes = [sds(fw, F32), sds(fw, F32), sds(fw, BF16)]
    else:
        b, l = seqs
        nt = l // tm
        assert b * l == n and l % tm == 0
        kv_specs = [pl.BlockSpec((None, fw, tm), lambda i: (i // nt, 0, i % nt))] * 3
        kv_shapes = [jax.ShapeDtypeStruct((b, fw, l), dt) for dt in (F32, F32, BF16)]
    return pl.pallas_call(
        functools.partial(_inproj_body, fw=fw, hw=hw, hvw=hvw, dh=dh, layer=layer, seq_major=seqs is not None),
        grid=(n // tm,),
        in_specs=[row(d), _resident((1, d)), _resident(w_mix.shape), _resident(w_vt.shape), _resident(gsum.shape),
                  _resident((1, fw)), _resident((1, fw)), _resident((1, V7X_LANES)),
                  _resident(lb_logits.shape)],
        out_specs=[row(fw), kv_specs[0], kv_specs[1], row(fw), kv_specs[2], row(V7X_LANES),
                   row(hw), row(hw), row(hw), row(hvw), row(hvw)],
        out_shape=[sds(fw, BF16), kv_shapes[0], kv_shapes[1], sds(fw, BF16), kv_shapes[2],
                   sds(V7X_LANES, F32),
                   sds(hw, F32), sds(hw, F32), sds(hw, F32), sds(hvw, BF16), sds(hvw, F32)],
        compiler_params=_params("parallel"),
        name="inproj",
    )(x, gain, w_mix, w_vt, gsum, q_gain, k_gain, f_bias, lb_logits)


BIAS_PIECES = 3


def _bias_lane(hh, dh):
    return ((hh + 1) % (V7X_LANES // dh)) * dh


def _cumsum_body(lf_ref, k_ref, tril_ref, place_ref, c_ref, kaug_ref, carry_ref, *, nh, dh):
    @pl.when(pl.program_id(1) == 0)
    def _():
        carry_ref[...] = jnp.zeros_like(carry_ref)

    c = _dot01(tril_ref[...], lf_ref[...]) + carry_ref[...]
    c_ref[...] = c
    carry_ref[...] = c[-1:, :]
    pieces = jnp.concatenate(_split3(-LOG2E * c), axis=1)
    lane = lax.broadcasted_iota(jnp.int32, (1, V7X_LANES), 1)
    hp = V7X_LANES // dh
    placed = _dot(pieces, place_ref[...]).astype(BF16)
    for h in range(nh):
        tile = h // hp
        keys = k_ref[:, tile * V7X_LANES:(tile + 1) * V7X_LANES]
        kaug_ref[h] = jnp.where((lane // dh) == (h % hp), keys, placed[:, h * V7X_LANES:(h + 1) * V7X_LANES])


def _cumsum(lf, k, tril, *, nh, dh):
    b, l, w = lf.shape
    tc = tril.shape[0]
    nt = l // tc
    place = np.zeros((BIAS_PIECES * w, nh * V7X_LANES), np.float32)
    for h in range(nh):
        for piece in range(BIAS_PIECES):
            place[piece * w + h, h * V7X_LANES + _bias_lane(h % (V7X_LANES // dh), dh) + piece] = 1.0
    place = jnp.asarray(place, BF16)
    return pl.pallas_call(
        functools.partial(_cumsum_body, nh=nh, dh=dh),
        grid=(b, nt),
        in_specs=[pl.BlockSpec((None, tc, w), lambda i, j: (i, j, 0)),
                  pl.BlockSpec((tc, k.shape[1]), lambda i, j: (i * nt + j, 0)),
                  _resident(tril.shape), _resident(place.shape)],
        out_specs=[pl.BlockSpec((None, tc, w), lambda i, j: (i, j, 0)),
                   pl.BlockSpec((None, nh, tc, V7X_LANES), lambda i, j: (i, 0, j, 0))],
        out_shape=[jax.ShapeDtypeStruct((b, l, w), F32), jax.ShapeDtypeStruct((b, nh, l, V7X_LANES), BF16)],
        scratch_shapes=[pltpu.VMEM((1, w), F32)],
        compiler_params=_params("parallel", "arbitrary"),
        name="logf_cumsum",
    )(lf, k, tril, place)


def _attn_body(q_ref, kaug_ref, vt_ref, crow_ref, o_ref, m_ref, acc_ref, u_ref, *, t, dh):
    p = pl.program_id(1)
    qi = pl.program_id(2)
    hp = V7X_LANES // dh
    q = q_ref[...]
    lane = lax.broadcasted_iota(jnp.int32, (1, V7X_LANES), 1)
    key_i = lax.broadcasted_iota(jnp.int32, (t, t), 0)
    qry_i = lax.broadcasted_iota(jnp.int32, (t, t), 1)
    feat = lax.broadcasted_iota(jnp.int32, (V7X_LANES, t), 0)
    qas, cts, keep, ones = [], [], [], []
    for hh in range(hp):
        b0 = _bias_lane(hh, dh)
        one_lanes = jnp.where((lane >= b0) & (lane < b0 + BIAS_PIECES), 1.0, 0.0).astype(BF16)
        qas.append(jnp.where((lane // dh) == hh, q, one_lanes))
        cts.append(LOG2E * crow_ref[pl.ds(p * hp + hh, 1), :])
        keep.append(jnp.where((feat // dh) == hh, 1.0, 0.0).astype(BF16))
        ones.append(jnp.where(feat == b0, 1.0, 0.0).astype(BF16))
        m_ref[hh] = jnp.full(m_ref.shape[1:], NEG_INF, F32)
        acc_ref[hh] = jnp.zeros(acc_ref.shape[1:], F32)

    heads = range(hp)

    def scores(kb):
        r0 = pl.multiple_of(kb * t, t)
        return [_dot_nt(kaug_ref[hh, pl.ds(r0, t), :], qas[hh]) for hh in heads]

    def prefetch_scores(kb, slot):
        for hh, u in enumerate(scores(kb)):
            u_ref[slot, hh] = u

    def steps(group, kbs, diagonal_last):
        slot = group % 2
        r0s = [pl.multiple_of(kb * t, t) for kb in kbs]
        us = [[u_ref[slot, hh] for hh in heads]] + [scores(kb) for kb in kbs[1:]]
        if not diagonal_last:
            prefetch_scores(kbs[-1] + 1, 1 - slot)
        for n, (r0, u_heads) in enumerate(zip(r0s, us)):
            vt = vt_ref[:, pl.ds(r0, t)]
            if diagonal_last and n == len(kbs) - 1:
                u_heads = [jnp.where(key_i <= qry_i, u, NEG_INF) for u in u_heads]
            m_old = [m_ref[hh] for hh in heads]
            m_new = [jnp.maximum(m_old[hh], jnp.max(u_heads[hh], axis=0, keepdims=True) + cts[hh]) for hh in heads]
            pe = [jnp.exp2(u_heads[hh] - (m_new[hh] - cts[hh])).astype(BF16) for hh in heads]
            pv = [_dot(vt * keep[hh] + ones[hh], pe[hh]) for hh in heads]
            for hh in heads:
                acc_ref[hh] = jnp.exp2(m_old[hh] - m_new[hh]) * acc_ref[hh] + pv[hh]
                m_ref[hh] = m_new[hh]

    def body(i, carry):
        steps(i, [ATTN_UNROLL * i + r for r in range(ATTN_UNROLL)], False)
        return carry

    prefetch_scores(0, 0)
    lax.fori_loop(0, qi // ATTN_UNROLL, body, 0)
    for rem in range(ATTN_UNROLL):
        @pl.when(qi % ATTN_UNROLL == rem)
        def _(rem=rem):
            steps(qi // ATTN_UNROLL, [qi - rem + r for r in range(rem)] + [qi], True)
    o_t = None
    for hh in range(hp):
        b0 = _bias_lane(hh, dh)
        acc = acc_ref[hh]
        o_h = acc / acc[b0:b0 + 1, :]
        o_t = o_h if o_t is None else jnp.where((feat // dh) == hh, o_h, o_t)
    o_ref[...] = o_t.T.astype(o_ref.dtype)


def _attn_prompt(q, kaug, vt, crow, *, b, l, dh):
    n, fw = q.shape
    t = min(ATTN_TILE, l)
    nq = l // t
    hp = V7X_LANES // dh
    assert l % t == 0 and fw % V7X_LANES == 0 and V7X_LANES % dh == 0 and BIAS_PIECES < dh
    return pl.pallas_call(
        functools.partial(_attn_body, t=t, dh=dh),
        grid=(b, fw // V7X_LANES, nq),
        in_specs=[pl.BlockSpec((t, V7X_LANES), lambda i, p, j: (i * nq + j, p)),
                  pl.BlockSpec((None, hp, l, V7X_LANES), lambda i, p, j: (i, p, 0, 0)),
                  pl.BlockSpec((None, V7X_LANES, l), lambda i, p, j: (i, p, 0)),
                  pl.BlockSpec((None, crow.shape[1], t), lambda i, p, j: (i, 0, j))],
        out_specs=pl.BlockSpec((t, V7X_LANES), lambda i, p, j: (i * nq + j, p)),
        out_shape=jax.ShapeDtypeStruct((n, fw), BF16),
        scratch_shapes=[pltpu.VMEM((hp, 1, t), F32), pltpu.VMEM((hp, V7X_LANES, t), F32),
                        pltpu.VMEM((2, hp, t, t), F32)],
        compiler_params=_params("parallel", "parallel", "arbitrary"),
        name="fox_prompt",
    )(q, kaug, vt, crow)


def _gla_masks(c):
    r = np.arange(c)[:, None]
    s = np.arange(c)[None, :]
    tril = (s <= r).astype(np.float32)
    levels = []
    m = c // 2
    while m >= GLA_LEAF:
        levels.append((((r ^ s) < 2 * m) & ((r & m) != 0) & ((s & m) == 0)).astype(np.float32))
        m //= 2
    if not levels:
        levels.append(np.zeros((c, c), np.float32))
    return jnp.asarray(tril, BF16), jnp.asarray(np.stack(levels), F32)


def _gla_chunk(q, k, lf, v, states, tril, lvl_ref, leaf_refs, *, c, nh, kd, vd):
    def head(x, h, w):
        return x[:, h * w:(h + 1) * w]

    b = _dot01(tril, LOG2E * lf)
    q = q * (kd ** -0.5)
    b_last = b[c - 1:c, :]
    q_hat = (q * jnp.exp2(b)).astype(BF16)
    k_hat = (k * jnp.exp2(b_last - b)).astype(BF16)
    decay_last = jnp.exp2(b_last)
    outs = [_dot_nt(head(q_hat, h, kd), states[h].astype(BF16)) for h in range(nh)]
    new_states = [states[h] * head(decay_last, h, kd) + _dot_tn(head(v, h, vd), head(k_hat, h, kd))
                  for h in range(nh)]

    m = c // 2
    lvl = 0
    scores = [None] * nh
    while m >= GLA_LEAF:
        pieces = []
        for blk in range(c // (2 * m)):
            mid = blk * 2 * m + m
            pieces.append(jnp.broadcast_to(b[mid - 1:mid, :], (2 * m, b.shape[1])))
        ref = pieces[0] if len(pieces) == 1 else jnp.concatenate(pieces, axis=0)
        e = jnp.exp2(-jnp.abs(b - ref))
        qe = (q * e).astype(BF16)
        ke = (k * e).astype(BF16)
        mask = lvl_ref[lvl]
        for h in range(nh):
            part = mask * _dot_nt(head(qe, h, kd), head(ke, h, kd))
            scores[h] = part if scores[h] is None else scores[h] + part
        m //= 2
        lvl += 1
    if scores[0] is not None:
        outs = [outs[h] + _dot(scores[h].astype(BF16), head(v, h, vd)) for h in range(nh)]

    nl = c // GLA_LEAF
    b_ref, q_ref, k_ref, v_ref, o_ref = leaf_refs
    v32 = v.astype(F32)
    for h in range(nh):
        b_ref[h] = head(b, h, kd)
        q_ref[h] = head(q, h, kd)
        k_ref[h] = head(k, h, kd)
        v_ref[h] = head(v32, h, vd)

    def rows(ref, i):
        return jnp.concatenate([ref[h, pl.ds(i, nl, stride=GLA_LEAF), :] for h in range(nh)], axis=1)

    bs = [rows(b_ref, i) for i in range(GLA_LEAF)]
    ks = [rows(k_ref, i) for i in range(GLA_LEAF)]
    vs = [rows(v_ref, i) for i in range(GLA_LEAF)]
    for t in range(GLA_LEAF):
        q_t = rows(q_ref, t)
        o_t = [None] * nh
        for s in range(t + 1):
            prod = jnp.exp2(bs[t] - bs[s]) * q_t * ks[s]
            for h in range(nh):
                w = jnp.sum(head(prod, h, kd), axis=-1, keepdims=True)
                term = w * head(vs[s], h, vd)
                o_t[h] = term if o_t[h] is None else o_t[h] + term
        for h in range(nh):
            o_ref[h, pl.ds(t, nl, stride=GLA_LEAF), :] = o_t[h]
    return jnp.concatenate([outs[h] + o_ref[h] for h in range(nh)], axis=1), new_states


def _gla_body(q_ref, k_ref, lf_ref, v_ref, s0_ref, tril_ref, lvl_ref, o_ref, sT_ref, st_ref, *leaf_refs,
              c, rows, nh, kd, vd):
    j = pl.program_id(1)

    @pl.when(j == 0)
    def _():
        st_ref[...] = s0_ref[...]

    tril = tril_ref[...]

    def chunk(ci, carry):
        rs = pl.ds(pl.multiple_of(ci * c, c), c)
        o, new_states = _gla_chunk(q_ref[rs, :], k_ref[rs, :], lf_ref[rs, :], v_ref[rs, :],
                                   [st_ref[h] for h in range(nh)], tril, lvl_ref, leaf_refs,
                                   c=c, nh=nh, kd=kd, vd=vd)
        o_ref[rs, :] = o
        for h in range(nh):
            st_ref[h] = new_states[h]
        return carry

    if rows == c:
        chunk(0, 0)
    else:
        lax.fori_loop(0, rows // c, chunk, 0, unroll=True)

    @pl.when(j == pl.num_programs(1) - 1)
    def _():
        sT_ref[...] = st_ref[...]


def _gla(q, k, lf, v, s0_t, *, b, l):
    n, hw = q.shape
    hvw = v.shape[1]
    nh, vd, kd = s0_t.shape[1:]
    c = min(GLA_CHUNK, l)
    rows = min(GLA_ROWS, l)
    assert l % rows == 0 and rows % c == 0 and c % GLA_LEAF == 0
    tril, lvls = _gla_masks(c)
    nr = l // rows
    row = lambda w: pl.BlockSpec((rows, w), lambda i, j: (i * nr + j, 0))
    state = pl.BlockSpec((None, nh, vd, kd), lambda i, j: (i, 0, 0, 0))
    return pl.pallas_call(
        functools.partial(_gla_body, c=c, rows=rows, nh=nh, kd=kd, vd=vd),
        grid=(b, nr),
        in_specs=[row(hw), row(hw), row(hw), row(hvw), state, _resident(tril.shape), _resident(lvls.shape)],
        out_specs=[row(hvw), state],
        out_shape=[jax.ShapeDtypeStruct((n, hvw), F32), jax.ShapeDtypeStruct(s0_t.shape, F32)],
        scratch_shapes=[pltpu.VMEM((nh, vd, kd), F32)]
        + [pltpu.VMEM((nh, c, w), F32) for w in (kd, kd, kd, vd, vd)],
        compiler_params=_params("parallel", "arbitrary"),
        name="hgrn2",
    )(q, k, lf, v, s0_t, tril, lvls)


def _merge_body(x_ref, g_ref, wg_ref, of_ref, oh_ref, hg_ref, og_ref, gsum_ref, wpf_ref, wph_ref, wo_ref,
                o_ref, *, d, vd):
    x = x_ref[...]
    h = _rms_rows(x, g_ref[...]).astype(BF16)
    gate_fox = _sigmoid(_dot(h, wg_ref[:, :d]))
    gate_hg = _sigmoid(_dot(h, wg_ref[:, d:]))
    y_fox = _dot(of_ref[...], wpf_ref[...])
    oh = oh_ref[...]
    ms = _group_sums((oh * oh).astype(BF16), gsum_ref[...]) * (1.0 / vd)
    hg = hg_ref[...]
    ob = (oh * lax.rsqrt(ms + EPS) * og_ref[...]) * (hg * _sigmoid(hg))
    y_hg = _dot(ob.astype(BF16), wph_ref[...])
    y = gate_fox * y_fox + gate_hg * y_hg
    o_ref[...] = x + _dot(y.astype(BF16), wo_ref[...])


def _merge(x, gain, w_gate, o_fox, o_hg, hgate, o_gain, gsum, w_pf, w_ph, w_out, *, vd):
    n, d = x.shape
    tm = min(ROW_TILE, n)
    assert n % tm == 0
    row = lambda w: pl.BlockSpec((tm, w), lambda i: (i, 0))
    return pl.pallas_call(
        functools.partial(_merge_body, d=d, vd=vd),
        grid=(n // tm,),
        in_specs=[row(d), _resident((1, d)), _resident(w_gate.shape), row(o_fox.shape[1]), row(o_hg.shape[1]),
                  row(hgate.shape[1]), _resident(o_gain.shape), _resident(gsum.shape),
                  _resident(w_pf.shape), _resident(w_ph.shape), _resident(w_out.shape)],
        out_specs=row(d),
        out_shape=jax.ShapeDtypeStruct((n, d), F32),
        compiler_params=_params("parallel"),
        name="merge",
    )(x, gain, w_gate, o_fox, o_hg, hgate, o_gain, gsum, w_pf, w_ph, w_out)


def _paged_body(pt_ref, *refs, g, ps, nh, dh, t):
    k_refs = refs[:g]
    v_refs = refs[g:2 * g]
    lf_refs = refs[2 * g:3 * g]
    (w_ref, q_ref, kn_ref, vn_ref, lfn_ref, o_ref,
     qbd_ref, ctcol_ref, carry_ref, m_ref, l_ref, acc_ref) = refs[3 * g:]
    j = pl.program_id(1)
    rows = nh * t
    fw = nh * dh
    row_head = lax.broadcasted_iota(jnp.int32, (rows, 1), 0) // t

    def new_token_cumsum():
        lfn = lfn_ref[...]
        sub = lax.broadcasted_iota(jnp.int32, lfn.shape, 0)
        ct = jnp.zeros_like(lfn)
        for s in range(t):
            ct = ct + jnp.where(sub >= s, lfn[s:s + 1, :], 0.0)
        return ct

    @pl.when(j == 0)
    def _():
        lane_head = lax.broadcasted_iota(jnp.int32, (rows, fw), 1) // dh
        q_rep = jnp.concatenate([q_ref[...]] * nh, axis=0)
        qbd_ref[...] = jnp.where(lane_head == row_head, q_rep, jnp.zeros_like(q_rep))
        ct_rep = jnp.concatenate([new_token_cumsum()] * nh, axis=0)
        lane = lax.broadcasted_iota(jnp.int32, ct_rep.shape, 1)
        ctcol_ref[...] = LOG2E * jnp.sum(jnp.where(lane == row_head, ct_rep, 0.0), axis=-1, keepdims=True)
        carry_ref[...] = jnp.zeros_like(carry_ref)
        m_ref[...] = jnp.full_like(m_ref, NEG_INF)
        l_ref[...] = jnp.zeros_like(l_ref)
        acc_ref[...] = jnp.zeros_like(acc_ref)

    qbd = qbd_ref[...]
    ctcol = ctcol_ref[...]

    def update(u, pv_fn):
        m_old = m_ref[...]
        m_new = jnp.maximum(m_old, jnp.max(u, axis=-1, keepdims=True) + ctcol)
        pe = jnp.exp2(u - (m_new - ctcol))
        alpha = jnp.exp2(m_old - m_new)
        l_ref[...] = alpha * l_ref[...] + jnp.sum(pe, axis=-1, keepdims=True)
        acc_ref[...] = alpha * acc_ref[...] + pv_fn(pe)
        m_ref[...] = m_new

    lf_all = jnp.concatenate([lf_refs[i][...] for i in range(g)], axis=0)
    sums = _dot01_r(lf_all, w_ref[...])
    carry = carry_ref[...]
    biases = [None] * g
    for i in reversed(range(g)):
        biases[i] = sums[i * nh:(i + 1) * nh, :ps] + carry
        carry = carry + sums[i * nh:(i + 1) * nh, ps:]
    carry_ref[...] = carry
    bias = LOG2E * jnp.concatenate(biases, axis=1)
    bias = jnp.broadcast_to(bias[:, None, :], (nh, t, g * ps)).reshape(rows, g * ps)

    s = jnp.concatenate([_dot(qbd, k_refs[i][...].astype(BF16)) for i in range(g)], axis=1)

    def pv_past(pe):
        out = None
        for i in range(g):
            part = _dot_nt(pe[:, i * ps:(i + 1) * ps].astype(BF16), v_refs[i][...].astype(BF16))
            out = part if out is None else out + part
        return out

    update(s + bias, pv_past)

    @pl.when(j == pl.num_programs(1) - 1)
    def _():
        pad = jnp.zeros((ps - t, fw), BF16)
        kn = jnp.concatenate([kn_ref[...], pad], axis=0)
        vn = jnp.concatenate([vn_ref[...], pad], axis=0)
        ct = new_token_cumsum()
        ct_pad = jnp.concatenate([ct, jnp.zeros((ps - t, ct.shape[1]), F32)], axis=0)
        lane = lax.broadcasted_iota(jnp.int32, (rows, ct.shape[1]), 1)
        onehot = jnp.where(lane == row_head, 1.0, 0.0).astype(BF16)
        c_s = _dot01_nt(onehot, ct_pad)
        col = lax.broadcasted_iota(jnp.int32, (rows, ps), 1)
        row_t = lax.broadcasted_iota(jnp.int32, (rows, ps), 0) % t
        u = jnp.where(col <= row_t, _dot_nt(qbd, kn) - LOG2E * c_s, NEG_INF)
        update(u, lambda pe: _dot(pe.astype(BF16), vn))
        o = acc_ref[...] / l_ref[...]
        lane_head = lax.broadcasted_iota(jnp.int32, (rows, fw), 1) // dh
        o = jnp.where(lane_head == row_head, o, 0.0)
        o_ref[...] = jnp.sum(o.reshape(nh, t, fw), axis=0).astype(o_ref.dtype)


def _attn_paged(cache_kt, cache_vt, cache_lft, page_table, q, k_new, v_new, lf_new, *, layer, t, dh):
    _, _, fw, ps = cache_kt.shape
    nb, n_pages = page_table.shape
    nh = fw // dh
    g = min(PAGES_PER_STEP, n_pages)
    ng = n_pages // g
    rows = nh * t
    assert n_pages % g == 0 and t <= ps and cache_lft.shape[2:] == (nh, ps)
    j_idx = np.arange(ps)
    w = np.concatenate([(j_idx[:, None] > j_idx[None, :]), np.ones((ps, ps), bool)], axis=1)
    w = jnp.asarray(w.astype(np.float32), BF16)

    def page(i, width):
        return pl.BlockSpec((None, None, width, ps),
                            lambda b, j, pt: (layer, pt[b, (ng - 1 - j) * g + i], 0, 0))

    new_rows = lambda width: pl.BlockSpec((t, width), lambda b, j, pt: (b, 0))
    grid_spec = pltpu.PrefetchScalarGridSpec(
        num_scalar_prefetch=1,
        grid=(nb, ng),
        in_specs=[page(i, fw) for i in range(g)] + [page(i, fw) for i in range(g)]
        + [page(i, nh) for i in range(g)]
        + [pl.BlockSpec(w.shape, lambda b, j, pt: (0, 0)),
           new_rows(fw), new_rows(fw), new_rows(fw), new_rows(lf_new.shape[1])],
        out_specs=new_rows(fw),
        scratch_shapes=[pltpu.VMEM((rows, fw), BF16), pltpu.VMEM((rows, 1), F32), pltpu.VMEM((nh, ps), F32),
                        pltpu.VMEM((rows, 1), F32), pltpu.VMEM((rows, 1), F32), pltpu.VMEM((rows, fw), F32)],
    )
    return pl.pallas_call(
        functools.partial(_paged_body, g=g, ps=ps, nh=nh, dh=dh, t=t),
        grid_spec=grid_spec,
        out_shape=jax.ShapeDtypeStruct((nb * t, fw), BF16),
        compiler_params=_params("parallel", "arbitrary"),
        name="fox_paged",
    )(page_table, *([cache_kt] * g), *([cache_vt] * g), *([cache_lft] * g), w, q, k_new, v_new, lf_new)


def _block_ones(width, group):
    i = np.arange(width) // group
    return jnp.asarray((i[:, None] == i[None, :]).astype(np.float32), BF16)


def kernel(x_prompt, x_sample, cache_k, cache_v, cache_logf, state_hgrn, page_table, ffn1_norm, ffn1_w_in, ffn1_w_out, mix_norm, w_in, fox_f_bias, fox_q_gain, fox_k_gain, hg_lb_logits, hg_o_gain, w_proj_fox, w_proj_hg, w_out, ffn2_norm, ffn2_w_in, ffn2_w_out):
    depth = w_in.shape[0]
    bp, lp, d = x_prompt.shape
    bs, ls, _ = x_sample.shape
    nh, dh = cache_k.shape[-2:]
    fw = nh * dh
    hg_heads, kd, vd = state_hgrn.shape[-3:]
    hw, hvw = hg_heads * kd, hg_heads * vd
    assert nh <= V7X_LANES and w_in.shape[-1] == 3 * fw + nh + 2 * hw + 2 * hvw + 2 * d

    assert fw % V7X_MXU_WIDTH == 0 and hvw % V7X_MXU_WIDTH == 0
    gsum_fox = _block_ones(V7X_MXU_WIDTH, dh)
    gsum_hg = _block_ones(V7X_MXU_WIDTH, vd)
    tc = min(ROW_TILE, lp)
    tril_seq = jnp.asarray(np.tril(np.ones((tc, tc), np.float32)), BF16)
    lb_logits = hg_lb_logits.astype(F32)
    cache_kt = jnp.transpose(cache_k, (0, 1, 3, 4, 2)).reshape(cache_k.shape[:2] + (fw, cache_k.shape[2]))
    cache_vt = jnp.transpose(cache_v, (0, 1, 3, 4, 2)).reshape(cache_v.shape[:2] + (fw, cache_v.shape[2]))
    cache_lft = jnp.transpose(cache_logf, (0, 1, 3, 2)).astype(F32)

    yp = x_prompt.reshape(bp * lp, d)
    ys = x_sample.reshape(bs * ls, d)
    outs = {name: [] for name in ("kp", "vp", "fp", "sp", "ks", "vs", "fs", "ss")}

    for l in range(depth):
        c_mix = 3 * fw + nh
        w_l = w_in[l]
        w_mix = jnp.concatenate(
            [w_l[:, :3 * fw], w_l[:, c_mix:c_mix + 2 * hw + 2 * hvw],
             jnp.pad(w_l[:, 3 * fw:c_mix], ((0, 0), (0, V7X_LANES - nh)))], axis=1).astype(BF16)
        w_gate = w_l[:, c_mix + 2 * hw + 2 * hvw:].astype(BF16)
        f_bias = jnp.pad(fox_f_bias[l], (0, V7X_LANES - nh)).reshape(1, V7X_LANES)
        q_gain = jnp.tile(fox_q_gain[l], nh).reshape(1, fw)
        k_gain = jnp.tile(fox_k_gain[l], nh).reshape(1, fw)
        o_gain = jnp.tile(hg_o_gain[l], hg_heads).reshape(1, hvw)
        mix_gain = mix_norm[l].reshape(1, d)
        w1_in, w1_out = ffn1_w_in[l].astype(BF16), ffn1_w_out[l].astype(BF16)
        w2_in, w2_out = ffn2_w_in[l].astype(BF16), ffn2_w_out[l].astype(BF16)
        w_pf, w_ph, w_o = w_proj_fox[l].astype(BF16), w_proj_hg[l].astype(BF16), w_out[l].astype(BF16)

        yp = _ffn(yp, ffn1_norm[l].reshape(1, d), w1_in, w1_out)
        ys = _ffn(ys, ffn1_norm[l].reshape(1, d), w1_in, w1_out)

        w_vt = jnp.transpose(w_l[:, 2 * fw:3 * fw]).astype(BF16)
        inproj = functools.partial(_inproj, gain=mix_gain, w_mix=w_mix, w_vt=w_vt, gsum=gsum_fox, q_gain=q_gain,
                                   k_gain=k_gain, f_bias=f_bias, lb_logits=lb_logits,
                                   fw=fw, hw=hw, hvw=hvw, dh=dh, layer=l)
        merge = functools.partial(_merge, gain=mix_gain, w_gate=w_gate, o_gain=o_gain, gsum=gsum_hg,
                                  w_pf=w_pf, w_ph=w_ph, w_out=w_o, vd=vd)

        q, kf, vf, kb, v_t, lf, hq, hk, hlf, hv, hgate = inproj(yp, seqs=(bp, lp))
        c, kaug = _cumsum(lf.reshape(bp, lp, V7X_LANES), kb, tril_seq, nh=nh, dh=dh)
        c_row = jnp.transpose(c[:, :, :nh], (0, 2, 1))
        o_fox = _attn_prompt(q, kaug, v_t, c_row, b=bp, l=lp, dh=dh)
        s0_t = jnp.zeros((bp, hg_heads, vd, kd), F32)
        o_hg, s_t = _gla(hq, hk, hlf, hv, s0_t, b=bp, l=lp)
        yp = merge(yp, o_fox=o_fox, o_hg=o_hg, hgate=hgate)
        outs["kp"].append(kf.reshape(bp, nh, dh, lp))
        outs["vp"].append(vf.reshape(bp, nh, dh, lp))
        outs["fp"].append(lf[:, :nh].reshape(bp, lp, nh).astype(cache_logf.dtype))
        outs["sp"].append(jnp.swapaxes(s_t, -1, -2).astype(x_prompt.dtype))

        q, kf, vf, kb, vb, lf, hq, hk, hlf, hv, hgate = inproj(ys)
        o_fox = _attn_paged(cache_kt, cache_vt, cache_lft, page_table, q, kb, vb, lf, layer=l, t=ls, dh=dh)
        s0_t = jnp.swapaxes(state_hgrn[l].astype(F32), -1, -2)
        o_hg, s_t = _gla(hq, hk, hlf, hv, s0_t, b=bs, l=ls)
        ys = merge(ys, o_fox=o_fox, o_hg=o_hg, hgate=hgate)
        outs["ks"].append(kf.reshape(bs, ls, nh, dh))
        outs["vs"].append(vf.reshape(bs, ls, nh, dh))
        outs["fs"].append(lf[:, :nh].reshape(bs, ls, nh).astype(cache_logf.dtype))
        outs["ss"].append(jnp.swapaxes(s_t, -1, -2).astype(state_hgrn.dtype))

        yp = _ffn(yp, ffn2_norm[l].reshape(1, d), w2_in, w2_out)
        ys = _ffn(ys, ffn2_norm[l].reshape(1, d), w2_in, w2_out)

    st = lambda name: jnp.stack(outs[name])
    seq_last = lambda name: jnp.transpose(st(name), (0, 1, 4, 2, 3))
    return (yp.reshape(bp, lp, d), ys.reshape(bs, ls, d), seq_last("kp"), seq_last("vp"), st("fp"), st("sp"),
            st("ks"), st("vs"), st("fs"), st("ss"))
```
